```python
import math
import jax, jax.numpy as jnp
from jax import lax
import numpy as np


D_MODEL = 1024
BATCH = 8
SEQ = 8192
DEPTH = 2

GRID_W = 64
CTX_LEN = 256
N_BRANCH = 4
BR_WIDTH = 256
MLA_HEADS = 4
MLA_NOPE = 64
MLA_ROPE = 32
MLA_V = 64
MLA_Q_LORA = 256
MLA_KV_LORA = 128
GMLP_GROUPS = 4
GMLP_CHUNK = 128
DIFF_HEADS = 4
DIFF_D = 32
DN_HEADS = 4
DN_DK = 64
DN_DV = 64
DN_CONV = 3
DN_CHUNK = 64

Q_BLOCK = 128
ROPE_BASE = 10000.0
LN_EPS = 1e-6
DEEPNORM_ALPHA = (2 * DEPTH) ** 0.25
DEEPNORM_BETA = (8 * DEPTH) ** -0.25

MLA_COLS = MLA_Q_LORA + MLA_KV_LORA + MLA_ROPE
GMLP_COLS = 2 * BR_WIDTH
DIFF_COLS = 3 * DIFF_HEADS * 2 * DIFF_D
DN_COLS = 3 * BR_WIDTH + 4 * DN_HEADS
GATE_COLS = N_BRANCH * BR_WIDTH
IN_COLS = MLA_COLS + GMLP_COLS + DIFF_COLS + DN_COLS + GATE_COLS
IN_SPLITS = [MLA_COLS, MLA_COLS + GMLP_COLS, MLA_COLS + GMLP_COLS + DIFF_COLS,
             MLA_COLS + GMLP_COLS + DIFF_COLS + DN_COLS]

kernel_name = "hybrid_mla_gmlp_diff_deltanet_prefix"


def layer_norm(x):
    xf = x.astype(jnp.float32)
    mu = jnp.mean(xf, -1, keepdims=True)
    var = jnp.mean(jnp.square(xf - mu), -1, keepdims=True)
    return ((xf - mu) * lax.rsqrt(var + LN_EPS)).astype(x.dtype)


def rms_norm(x, g):
    xf = x.astype(jnp.float32)
    return (xf * lax.rsqrt(jnp.mean(xf * xf, -1, keepdims=True) + LN_EPS)).astype(x.dtype) * g


def l2_normalize(x):
    xf = x.astype(jnp.float32)
    return (xf * lax.rsqrt(jnp.sum(xf * xf, -1, keepdims=True) + LN_EPS)).astype(x.dtype)


def axial_angles(n, rot_dim):
    rows = n // GRID_W
    row = jnp.repeat(jnp.arange(rows, dtype=jnp.float32), GRID_W)
    col = jnp.tile(jnp.arange(GRID_W, dtype=jnp.float32), rows)
    axis_dim = rot_dim // 2
    inv_freq = ROPE_BASE ** (-jnp.arange(0, axis_dim, 2, dtype=jnp.float32) / axis_dim)
    return row[:, None] * inv_freq, col[:, None] * inv_freq


def rope_1d(x, ang):
    cos = jnp.cos(ang)[None, :, None, :].astype(x.dtype)
    sin = jnp.sin(ang)[None, :, None, :].astype(x.dtype)
    x1, x2 = jnp.split(x, 2, axis=-1)
    return jnp.concatenate([x1 * cos - x2 * sin, x2 * cos + x1 * sin], -1)


def rope_2d(x, angles):
    ang_row, ang_col = angles
    xr, xc = jnp.split(x, 2, axis=-1)
    return jnp.concatenate([rope_1d(xr, ang_row), rope_1d(xc, ang_col)], -1)


def sweep_query_blocks(fn, *qs):
    b, t = qs[0].shape[:2]
    nb = t // Q_BLOCK
    blocks = tuple(jnp.swapaxes(q.reshape(b, nb, Q_BLOCK, *q.shape[2:]), 0, 1) for q in qs)
    out = lax.map(lambda blk: fn(*blk), blocks)
    return jnp.swapaxes(out, 0, 1).reshape(b, t, *out.shape[3:])


def softmax_probs(q, k, scale):
    s = jnp.einsum('bqhd,bkhd->bhqk', q, k).astype(jnp.float32) * scale
    return jax.nn.softmax(s, axis=-1)


def mla_project(p, q_norm, w_uq, kv_norm, w_ukv, angles):
    b, t = p.shape[:2]
    cq, ckv, kr = jnp.split(p, [MLA_Q_LORA, MLA_Q_LORA + MLA_KV_LORA], -1)
    q = (rms_norm(cq, q_norm) @ w_uq).reshape(b, t, MLA_HEADS, MLA_NOPE + MLA_ROPE)
    q_nope, q_rope = jnp.split(q, [MLA_NOPE], -1)
    kv = (rms_norm(ckv, kv_norm) @ w_ukv).reshape(b, t, MLA_HEADS, MLA_NOPE + MLA_V)
    k_nope, v = jnp.split(kv, [MLA_NOPE], -1)
    kr = kr[:, :, None, :]
    if angles is not None:
        q_rope = rope_2d(q_rope, angles)
        kr = rope_2d(kr, angles)
    q = jnp.concatenate([q_nope, q_rope], -1)
    k = jnp.concatenate([k_nope, jnp.broadcast_to(kr, (b, t, MLA_HEADS, MLA_ROPE))], -1)
    return q, k, v


def mla_branch(px, pc, angles, q_norm, w_uq, kv_norm, w_ukv, need_ctx):
    b, t = px.shape[:2]
    qx, kx, vx = mla_project(px, q_norm, w_uq, kv_norm, w_ukv, angles)
    qc, kc, vc = mla_project(pc, q_norm, w_uq, kv_norm, w_ukv, None)
    scale = (MLA_NOPE + MLA_ROPE) ** -0.5

    def attend(k, v):
        def block(qb):
            p = softmax_probs(qb, k, scale).astype(v.dtype)
            return jnp.einsum('bhqk,bkhd->bqhd', p, v)
        return block

    k_all = jnp.concatenate([kx, kc], 1)
    v_all = jnp.concatenate([vx, vc], 1)
    yx = sweep_query_blocks(attend(k_all, v_all), qx).reshape(b, t, BR_WIDTH)
    yc = sweep_query_blocks(attend(kc, vc), qc).reshape(b, pc.shape[1], BR_WIDTH) if need_ctx else None
    return yx, yc


def gmlp_branch(p, ln_g, w_s, b_s):
    b, t = p.shape[:2]
    z = jax.nn.gelu(p)
    u, v = jnp.split(z, 2, -1)
    v = layer_norm(v) * ln_g
    v = v.reshape(b, t // GMLP_CHUNK, GMLP_CHUNK, GMLP_GROUPS, BR_WIDTH // GMLP_GROUPS)
    v = jnp.einsum('gpq,bnqgc->bnpgc', w_s, v) + b_s.T[:, :, None]
    return u * v.reshape(b, t, BR_WIDTH)


def diff_project(p, angles):
    b, t = p.shape[:2]
    q, k, v = jnp.split(p, [2 * DIFF_HEADS * DIFF_D, 4 * DIFF_HEADS * DIFF_D], -1)
    q = q.reshape(b, t, DIFF_HEADS, 2, DIFF_D)
    k = k.reshape(b, t, DIFF_HEADS, 2, DIFF_D)
    q1, q2, k1, k2 = q[..., 0, :], q[..., 1, :], k[..., 0, :], k[..., 1, :]
    if angles is not None:
        q1, q2, k1, k2 = (rope_2d(a, angles) for a in (q1, q2, k1, k2))
    return q1, q2, k1, k2, v.reshape(b, t, DIFF_HEADS, 2 * DIFF_D)


def diff_branch(px, pc, angles, lq1, lk1, lq2, lk2, norm_g, lam_init, need_ctx):
    b, t = px.shape[:2]
    lam = (jnp.exp(jnp.sum((lq1 * lk1).astype(jnp.float32)))
           - jnp.exp(jnp.sum((lq2 * lk2).astype(jnp.float32))) + lam_init)
    scale = DIFF_D ** -0.5
    q1x, q2x, k1x, k2x, vx = diff_project(px, angles)
    q1c, q2c, k1c, k2c, vc = diff_project(pc, None)

    def attend(k1, k2, v):
        def block(q1b, q2b):
            p = softmax_probs(q1b, k1, scale) - lam * softmax_probs(q2b, k2, scale)
            return jnp.einsum('bhqk,bkhd->bqhd', p.astype(v.dtype), v)
        return block

    def finish(o):
        return (rms_norm(o, norm_g) * (1.0 - lam_init)).reshape(o.shape[0], o.shape[1], BR_WIDTH)

    k1_all = jnp.concatenate([k1x, k1c], 1)
    k2_all = jnp.concatenate([k2x, k2c], 1)
    v_all = jnp.concatenate([vx, vc], 1)
    yx = finish(sweep_query_blocks(attend(k1_all, k2_all, v_all), q1x, q2x))
    yc = finish(sweep_query_blocks(attend(k1c, k2c, vc), q1c, q2c)) if need_ctx else None
    return yx, yc


def short_conv(x, w):
    t = x.shape[1]
    pad = DN_CONV // 2
    xp = jnp.pad(x, ((0, 0), (pad, pad), (0, 0)))
    y = xp[:, 0:t] * w[0]
    for i in range(1, DN_CONV):
        y = y + xp[:, i:i + t] * w[i]
    return y


def dn_prep(p, conv_w, a_log, dt_bias):
    b, t = p.shape[:2]
    qkv, a, bb = jnp.split(p, [3 * BR_WIDTH, 3 * BR_WIDTH + 2 * DN_HEADS], -1)
    qkv = jax.nn.silu(short_conv(qkv, conv_w))
    q, k, v = jnp.split(qkv, 3, -1)
    q = l2_normalize(q.reshape(b, t, DN_HEADS, DN_DK)) * (DN_DK ** -0.5)
    k = l2_normalize(k.reshape(b, t, DN_HEADS, DN_DK))
    v = v.reshape(b, t, DN_HEADS, DN_DV)
    a = a.reshape(b, t, 2, DN_HEADS).astype(jnp.float32)
    bb = bb.reshape(b, t, 2, DN_HEADS).astype(jnp.float32)
    g = -jnp.exp(a_log.astype(jnp.float32)) * jax.nn.softplus(a + dt_bias.astype(jnp.float32))
    beta = jax.nn.sigmoid(bb)
    return q, k, v, g, beta


def gated_delta_chunked(q, k, v, g, beta, s0):
    b, t, h, _ = q.shape
    dv = v.shape[-1]
    n = t // DN_CHUNK

    def chunked(a):
        a = a.astype(jnp.float32).reshape(b, n, DN_CHUNK, h, *a.shape[3:])
        return jnp.moveaxis(a, (1, 3), (0, 2))

    qc, kc, vc, gc, bc = chunked(q), chunked(k), chunked(v), chunked(g), chunked(beta)
    gam = jnp.cumsum(gc, -1)
    idx = jnp.arange(DN_CHUNK)
    incl = idx[:, None] >= idx[None, :]
    strict = idx[:, None] > idx[None, :]
    decay = jnp.exp(jnp.where(incl, gam[..., :, None] - gam[..., None, :], -jnp.inf))
    kb = kc * bc[..., None]
    a_mat = jnp.where(strict, jnp.einsum('nbhid,nbhjd->nbhij', kb, kc) * decay, 0.0)
    m = a_mat + jnp.eye(DN_CHUNK, dtype=jnp.float32)
    u = lax.linalg.triangular_solve(m, vc * bc[..., None], left_side=True, lower=True, unit_diagonal=True)
    w = lax.linalg.triangular_solve(m, kb * jnp.exp(gam)[..., None], left_side=True, lower=True,
                                    unit_diagonal=True)
    qk = jnp.where(incl, jnp.einsum('nbhid,nbhjd->nbhij', qc, kc) * decay, 0.0)
    q_dec = qc * jnp.exp(gam)[..., None]
    k_dec = kc * jnp.exp(gam[..., -1:] - gam)[..., None]
    last = jnp.exp(gam[..., -1])

    def step(s, xs):
        u_i, w_i, qk_i, qd_i, kd_i, last_i = xs
        v_new = u_i - jnp.einsum('bhcd,bhde->bhce', w_i, s)
        o_i = jnp.einsum('bhcd,bhde->bhce', qd_i, s) + jnp.einsum('bhij,bhje->bhie', qk_i, v_new)
        s = s * last_i[..., None, None] + jnp.einsum('bhcd,bhce->bhde', kd_i, v_new)
        return s, o_i

    s, o = lax.scan(step, s0.astype(jnp.float32), (u, w, qk, q_dec, k_dec, last))
    o = jnp.moveaxis(o, (0, 2), (1, 3)).reshape(b, t, h, dv)
    return o, s


def run_direction(q, k, v, g, beta, s0, d):
    if d == 0:
        return gated_delta_chunked(q, k, v, g[:, :, 0], beta[:, :, 0], s0)
    fl = lambda a: jnp.flip(a, 1)
    o, s = gated_delta_chunked(fl(q), fl(k), fl(v), fl(g[:, :, 1]), fl(beta[:, :, 1]), s0)
    return fl(o), s


def deltanet_branch(px, pc, conv_w, a_log, dt_bias, norm_g, need_ctx):
    b, t = px.shape[:2]
    qx, kx, vx, gx, bx = dn_prep(px, conv_w, a_log, dt_bias)
    qc, kc, vc, gcx, bcx = dn_prep(pc, conv_w, a_log, dt_bias)
    s0 = jnp.zeros((b, DN_HEADS, DN_DK, DN_DV), jnp.float32)
    oc_f, sc_f = run_direction(qc, kc, vc, gcx, bcx, s0, 0)
    oc_b, sc_b = run_direction(qc, kc, vc, gcx, bcx, s0, 1)
    ox_f, _ = run_direction(qx, kx, vx, gx, bx, sc_f, 0)
    ox_b, _ = run_direction(qx, kx, vx, gx, bx, sc_b, 1)

    def finish(o):
        return rms_norm(o.astype(px.dtype), norm_g).reshape(o.shape[0], o.shape[1], BR_WIDTH)

    yx = finish(ox_f + ox_b)
    yc = finish(oc_f + oc_b) if need_ctx else None
    return yx, yc


def setup_inputs(seed: int = 0) -> dict:
    key = jax.random.key(seed)
    ks = iter(jax.random.split(key, 40))
    L, D = DEPTH, D_MODEL

    def nrm(shape, scale):
        return jax.random.normal(next(ks), shape, jnp.float32) * scale

    def gain(shape):
        return 1.0 + nrm(shape, 0.05)

    x = nrm((BATCH, SEQ, D), 1.0)
    c = nrm((BATCH, D), 1.0)
    ctx = nrm((BATCH, CTX_LEN, D), 1.0)
    c_ctx = nrm((D,), 1.0)
    w_mod = nrm((L, D, 3 * D), 0.5 * D ** -0.5)
    b_mod = nrm((L, 3 * D), 0.01)
    w_in = nrm((L, D, IN_COLS), D ** -0.5)
    mla_q_norm = gain((L, MLA_Q_LORA))
    mla_w_uq = nrm((L, MLA_Q_LORA, MLA_HEADS * (MLA_NOPE + MLA_ROPE)), MLA_Q_LORA ** -0.5)
    mla_kv_norm = gain((L, MLA_KV_LORA))
    mla_w_ukv = nrm((L, MLA_KV_LORA, MLA_HEADS * (MLA_NOPE + MLA_V)), MLA_KV_LORA ** -0.5)
    gmlp_ln_g = gain((L, BR_WIDTH))
    gmlp_w_s = nrm((L, GMLP_GROUPS, GMLP_CHUNK, GMLP_CHUNK), GMLP_CHUNK ** -0.5)
    gmlp_b_s = 1.0 + nrm((L, GMLP_GROUPS, GMLP_CHUNK), 0.05)
    diff_lq1 = nrm((L, DIFF_D), 0.1)
    diff_lk1 = nrm((L, DIFF_D), 0.1)
    diff_lq2 = nrm((L, DIFF_D), 0.1)
    diff_lk2 = nrm((L, DIFF_D), 0.1)
    diff_norm_g = gain((L, 2 * DIFF_D))
    dn_conv_w = nrm((L, DN_CONV, 3 * BR_WIDTH), DN_CONV ** -0.5)
    dn_a_log = jnp.log(jax.random.uniform(next(ks), (L, 2, DN_HEADS), jnp.float32, 1.0, 16.0))
    dt = jnp.exp(jax.random.uniform(next(ks), (L, 2, DN_HEADS), jnp.float32,
                                    math.log(1e-3), math.log(1e-1)))
    dn_dt_bias = dt + jnp.log(-jnp.expm1(-dt))
    dn_norm_g = gain((L, DN_DV))
    w_gate = nrm((L, N_BRANCH, D, D), D ** -0.5)
    w_branch = nrm((L, N_BRANCH, BR_WIDTH, D), DEEPNORM_BETA * BR_WIDTH ** -0.5)
    w_out = nrm((L, D, D), DEEPNORM_BETA * D ** -0.5)
    ln_g = gain((L, D))
    ln_b = nrm((L, D), 0.01)
    return {"x": x, "c": c, "ctx": ctx, "c_ctx": c_ctx, "w_mod": w_mod, "b_mod": b_mod, "w_in": w_in,
            "mla_q_norm": mla_q_norm, "mla_w_uq": mla_w_uq, "mla_kv_norm": mla_kv_norm,
            "mla_w_ukv": mla_w_ukv, "gmlp_ln_g": gmlp_ln_g, "gmlp_w_s": gmlp_w_s, "gmlp_b_s": gmlp_b_s,
            "diff_lq1": diff_lq1, "diff_lk1": diff_lk1, "diff_lq2": diff_lq2, "diff_lk2": diff_lk2,
            "diff_norm_g": diff_norm_g, "dn_conv_w": dn_conv_w, "dn_a_log": dn_a_log,
            "dn_dt_bias": dn_dt_bias, "dn_norm_g": dn_norm_g, "w_gate": w_gate, "w_branch": w_branch,
            "w_out": w_out, "ln_g": ln_g, "ln_b": ln_b}


def reference(x, c, ctx, c_ctx, w_mod, b_mod, w_in, mla_q_norm, mla_w_uq, mla_kv_norm, mla_w_ukv,
              gmlp_ln_g, gmlp_w_s, gmlp_b_s, diff_lq1, diff_lk1, diff_lq2, diff_lk2, diff_norm_g,
              dn_conv_w, dn_a_log, dn_dt_bias, dn_norm_g, w_gate, w_branch, w_out, ln_g, ln_b):
    n = x.shape[1]
    ang_mla = axial_angles(n, MLA_ROPE)
    ang_diff = axial_angles(n, DIFF_D)
    for l in range(DEPTH):
        need_ctx = l < DEPTH - 1
        lam_init = 0.8 - 0.6 * math.exp(-0.3 * l)
        mod_x = jax.nn.silu(c) @ w_mod[l] + b_mod[l]
        mod_c = jax.nn.silu(c_ctx) @ w_mod[l] + b_mod[l]
        sh_x, sc_x, gt_x = jnp.split(mod_x[:, None, :], 3, -1)
        sh_c, sc_c, gt_c = jnp.split(mod_c, 3, -1)
        hx = layer_norm(x) * (1.0 + sc_x) + sh_x
        hc = layer_norm(ctx) * (1.0 + sc_c) + sh_c
        px_a, px_b, px_c, px_d, px_g = jnp.split(hx @ w_in[l], IN_SPLITS, -1)
        pc_a, pc_b, pc_c, pc_d, pc_g = jnp.split(hc @ w_in[l], IN_SPLITS, -1)

        ya_x, ya_c = mla_branch(px_a, pc_a, ang_mla, mla_q_norm[l], mla_w_uq[l], mla_kv_norm[l],
                                mla_w_ukv[l], need_ctx)
        yb_x = gmlp_branch(px_b, gmlp_ln_g[l], gmlp_w_s[l], gmlp_b_s[l])
        yb_c = gmlp_branch(pc_b, gmlp_ln_g[l], gmlp_w_s[l], gmlp_b_s[l]) if need_ctx else None
        yc_x, yc_c = diff_branch(px_c, pc_c, ang_diff, diff_lq1[l], diff_lk1[l], diff_lq2[l], diff_lk2[l],
                                 diff_norm_g[l], lam_init, need_ctx)
        yd_x, yd_c = deltanet_branch(px_d, pc_d, dn_conv_w[l], dn_a_log[l], dn_dt_bias[l], dn_norm_g[l],
                                     need_ctx)

        def merge(h, branches, pgate):
            silu_gates = jnp.split(jax.nn.silu(pgate), N_BRANCH, -1)
            acc = None
            for i in range(N_BRANCH):
                term = jax.nn.sigmoid(h @ w_gate[l, i]) * ((branches[i] * silu_gates[i]) @ w_branch[l, i])
                acc = term if acc is None else acc + term
            return acc @ w_out[l]

        out_x = merge(hx, (ya_x, yb_x, yc_x, yd_x), px_g)
        x_new = layer_norm(DEEPNORM_ALPHA * x + gt_x * out_x) * ln_g[l] + ln_b[l]
        if need_ctx:
            out_c = merge(hc, (ya_c, yb_c, yc_c, yd_c), pc_g)
            ctx = layer_norm(DEEPNORM_ALPHA * ctx + gt_c * out_c) * ln_g[l] + ln_b[l]
        x = x_new
    return x
```

```python
import functools
import math

import jax
import jax.numpy as jnp
from jax import lax
from jax.experimental import pallas as pl
from jax.experimental.pallas import tpu as pltpu

F32 = jnp.float32
BF16 = jnp.bfloat16

D_MODEL = 1024
GRID_W = 64
N_BRANCH = 4
BR_WIDTH = 256
MLA_HEADS = 4
MLA_NOPE = 64
MLA_ROPE = 32
MLA_V = 64
MLA_Q_LORA = 256
MLA_KV_LORA = 128
GMLP_GROUPS = 4
GMLP_CHUNK = 128
DIFF_HEADS = 4
DIFF_D = 32
DN_HEADS = 4
DN_DK = 64
DN_DV = 64
DN_CHUNK = 64
ROPE_BASE = 10000.0
LN_EPS = 1e-6

MLA_COLS = MLA_Q_LORA + MLA_KV_LORA + MLA_ROPE
GMLP_COLS = 2 * BR_WIDTH
DIFF_COLS = 3 * DIFF_HEADS * 2 * DIFF_D
DN_COLS = 3 * BR_WIDTH + 4 * DN_HEADS

LANES = 128
SUBLANES = 8
VMEM_LIMIT_BYTES = 56 * 1024 * 1024

TOKEN_TILE = 256
ATTN_Q_TILE = 256
ATTN_K_TILE = 256
NEG_BIG = -1e30

PA_COLS = 640
PB_COLS = 512
PC_COLS = 1280
PD_COLS = 768
PAB_COLS = 128
PG_COLS = 1024
PACK_SPLITS = (PA_COLS, PB_COLS, PC_COLS, PD_COLS, PAB_COLS, PG_COLS)
PACK_COLS = sum(PACK_SPLITS)


def _cparams(semantics):
    return pltpu.CompilerParams(dimension_semantics=semantics, vmem_limit_bytes=VMEM_LIMIT_BYTES)


def _dot(a, b):
    return jnp.dot(a, b, preferred_element_type=F32)


def _dot_nt(a, b):
    return lax.dot_general(a, b, (((1,), (1,)), ((), ())), preferred_element_type=F32)


def _dot_tn(a, b):
    return lax.dot_general(a, b, (((0,), (0,)), ((), ())), preferred_element_type=F32)


def _dot_f32(a, b):
    return jnp.dot(a, b, preferred_element_type=F32, precision=lax.Precision.HIGHEST)


def _sigmoid(x):
    return 1.0 / (1.0 + jnp.exp(-x))


def _iota(shape, axis):
    return lax.broadcasted_iota(jnp.int32, shape, axis)


def _block_ones(n, blk):
    return (_iota((n, n), 0) // blk == _iota((n, n), 1) // blk).astype(F32)


def _mod_body(c_ref, w_ref, b_ref, o_ref):
    c = c_ref[...]
    s = (c * _sigmoid(c)).astype(BF16)
    o_ref[0] = _dot(s, w_ref[0].astype(BF16)) + b_ref[0]


def _modulation(cc, w_mod, b_mod):
    depth, d, n3 = w_mod.shape
    rows = cc.shape[0]
    tn = 1024
    return pl.pallas_call(
        _mod_body,
        grid=(depth, n3 // tn),
        in_specs=[pl.BlockSpec((rows, d), lambda l, j: (0, 0)),
                  pl.BlockSpec((1, d, tn), lambda l, j: (l, 0, j)),
                  pl.BlockSpec((1, 1, tn), lambda l, j: (l, 0, j))],
        out_specs=pl.BlockSpec((1, rows, tn), lambda l, j: (l, 0, j)),
        out_shape=jax.ShapeDtypeStruct((depth, rows, n3), F32),
        compiler_params=_cparams(("arbitrary", "arbitrary")),
        name="modulation",
    )(cc, w_mod, b_mod.reshape(depth, 1, n3))


def _inproj_body(x_ref, mod_ref, w_ref, h_ref, *out_refs):
    x = x_ref[0]
    mu = jnp.mean(x, -1, keepdims=True)
    xc = x - mu
    var = jnp.mean(xc * xc, -1, keepdims=True)
    xn = xc * lax.rsqrt(var + LN_EPS)
    sh = mod_ref[0, 0, 0:1, :]
    sc = mod_ref[0, 0, 1:2, :]
    hb = (xn * (1.0 + sc) + sh).astype(BF16)
    h_ref[0] = hb
    off = 0
    for ref, width in zip(out_refs, PACK_SPLITS):
        ref[0] = _dot(hb, w_ref[:, off:off + width])
        off += width


def _inproj(xs, mod, w_pack, n_ctx_tiles):
    b, s, d = xs.shape
    tm = TOKEN_TILE
    tok = lambda width, dt: jax.ShapeDtypeStruct((b, s, width), dt)
    tok_spec = lambda width: pl.BlockSpec((1, tm, width), lambda i, j: (i, j, 0))
    return pl.pallas_call(
        _inproj_body,
        grid=(b, s // tm),
        in_specs=[tok_spec(d),
                  pl.BlockSpec((1, 1, 3, d), lambda i, j: (i, jnp.where(j >= n_ctx_tiles, 1, 0), 0, 0)),
                  pl.BlockSpec((d, PACK_COLS), lambda i, j: (0, 0))],
        out_specs=[tok_spec(d)] + [tok_spec(w) for w in PACK_SPLITS],
        out_shape=[tok(d, BF16)] + [tok(w, F32) for w in PACK_SPLITS],
        compiler_params=_cparams(("arbitrary", "arbitrary")),
        name="inproj",
    )(xs, mod, w_pack)


def _rms(x, g):
    return x * lax.rsqrt(jnp.mean(x * x, -1, keepdims=True) + LN_EPS) * g


def _mla_prep_body(pa_ref, cos_ref, sin_ref, qn_ref, kvn_ref, wq_ref, wqs_ref, wk_ref, wv_ref,
                   q_ref, k_ref, vt_ref, *, scale):
    pa = pa_ref[0]
    cq = pa[:, 0:MLA_Q_LORA]
    ckv = pa[:, MLA_Q_LORA:MLA_Q_LORA + MLA_KV_LORA]
    kr = pa[:, 384:512]
    kr_sw = pa[:, 512:640]
    cos = cos_ref[...]
    sin = sin_ref[...]
    cos4 = jnp.concatenate([cos] * MLA_HEADS, axis=1)
    sin4 = jnp.concatenate([sin] * MLA_HEADS, axis=1)
    cqn = _rms(cq, qn_ref[...]).astype(BF16)
    q = _dot(cqn, wq_ref[...])
    q_sw = _dot(cqn, wqs_ref[...])
    q_ref[0] = ((q * cos4 + q_sw * sin4) * scale).astype(BF16)
    ckvn = _rms(ckv, kvn_ref[...]).astype(BF16)
    kn = _dot(ckvn, wk_ref[...])
    kr_rot = kr * cos + kr_sw * sin
    k_ref[0] = (kn + jnp.concatenate([kr_rot] * MLA_HEADS, axis=1)).astype(BF16)
    v = _dot(ckvn, wv_ref[...])
    vt_ref[0, 0] = v.T.astype(BF16)


def _mla_prep(pa, cos_t, sin_t, qn, kvn, wq, wqs, wk, wv):
    b, s, _ = pa.shape
    tm = ATTN_K_TILE
    hp = MLA_HEADS * LANES
    full = lambda a: pl.BlockSpec(a.shape, lambda i, j: (0,) * a.ndim)
    scale = (MLA_NOPE + MLA_ROPE) ** -0.5
    return pl.pallas_call(
        functools.partial(_mla_prep_body, scale=scale),
        grid=(b, s // tm),
        in_specs=[pl.BlockSpec((1, tm, PA_COLS), lambda i, j: (i, j, 0)),
                  pl.BlockSpec((tm, LANES), lambda i, j: (j, 0)),
                  pl.BlockSpec((tm, LANES), lambda i, j: (j, 0)),
                  full(qn), full(kvn), full(wq), full(wqs), full(wk), full(wv)],
        out_specs=[pl.BlockSpec((1, tm, hp), lambda i, j: (i, j, 0)),
                   pl.BlockSpec((1, tm, hp), lambda i, j: (i, j, 0)),
                   pl.BlockSpec((1, 1, BR_WIDTH, tm), lambda i, j: (i, j, 0, 0))],
        out_shape=[jax.ShapeDtypeStruct((b, s, hp), BF16),
                   jax.ShapeDtypeStruct((b, s, hp), BF16),
                   jax.ShapeDtypeStruct((b, s // tm, BR_WIDTH, tm), BF16)],
        compiler_params=_cparams(("arbitrary", "arbitrary")),
        name="mla_prep",
    )(pa, cos_t, sin_t, qn, kvn, wq, wqs, wk, wv)


def _flash_head(q, k_ref, vt_ref, k_lanes, v_rows, n_k, tq):
    tk = ATTN_K_TILE
    dv = v_rows[1] - v_rows[0]

    def step(kt, carry):
        m, l, acc = carry
        ks = k_ref[0, pl.ds(pl.multiple_of(kt * tk, tk), tk), k_lanes[0]:k_lanes[1]]
        s = _dot_nt(ks, q)
        m_new = jnp.maximum(m, jnp.max(s, axis=0, keepdims=True))
        alpha = jnp.exp(m - m_new)
        p = jnp.exp(s - m_new)
        l = alpha * l + jnp.sum(p, axis=0, keepdims=True)
        vt = vt_ref[0, kt, v_rows[0]:v_rows[1], :]
        acc = alpha * acc + _dot(vt, p.astype(BF16))
        return m_new, l, acc

    init = (jnp.full((1, tq), NEG_BIG, F32), jnp.zeros((1, tq), F32), jnp.zeros((dv, tq), F32))
    _, l, acc = lax.fori_loop(0, n_k, step, init)
    return acc, l


def _mla_attn_body(q_ref, k_ref, vt_ref, o_ref, *, n_ctx_q_tiles, n_ctx_k_tiles, n_k_tiles):
    tq = q_ref.shape[1]
    j = pl.program_id(1)
    n_k = jnp.where(j < n_ctx_q_tiles, n_ctx_k_tiles, n_k_tiles)
    outs = []
    for h in range(MLA_HEADS):
        q = q_ref[0, :, h * LANES:(h + 1) * LANES]
        acc, l = _flash_head(q, k_ref, vt_ref, (h * LANES, (h + 1) * LANES),
                             (h * MLA_V, (h + 1) * MLA_V), n_k, tq)
        outs.append(acc / l)
    o_ref[0] = jnp.concatenate(outs, axis=0).T


def _mla_attn(q, k, vt, n_ctx):
    b, s, hp = q.shape
    tq, tk = ATTN_Q_TILE, ATTN_K_TILE
    return pl.pallas_call(
        functools.partial(_mla_attn_body, n_ctx_q_tiles=n_ctx // tq, n_ctx_k_tiles=n_ctx // tk,
                          n_k_tiles=s // tk),
        grid=(b, s // tq),
        in_specs=[pl.BlockSpec((1, tq, hp), lambda i, j: (i, j, 0)),
                  pl.BlockSpec((1, s, hp), lambda i, j: (i, 0, 0)),
                  pl.BlockSpec((1, s // tk, BR_WIDTH, tk), lambda i, j: (i, 0, 0, 0))],
        out_specs=pl.BlockSpec((1, tq, BR_WIDTH), lambda i, j: (i, j, 0)),
        out_shape=jax.ShapeDtypeStruct((b, s, BR_WIDTH), F32),
        compiler_params=_cparams(("arbitrary", "arbitrary")),
        name="mla_attn",
    )(q, k, vt)


def _diff_prep_body(pc_ref, cos_ref, sin_ref, q_ref, k_ref, vt_ref, *, scale):
    pc = pc_ref[0]
    cos = jnp.concatenate([cos_ref[...]] * 2, axis=1)
    sin = jnp.concatenate([sin_ref[...]] * 2, axis=1)
    q = pc[:, 0:256]
    k = pc[:, 256:512]
    v = pc[:, 512:768]
    q_sw = pc[:, 768:1024]
    k_sw = pc[:, 1024:1280]
    q_ref[0] = ((q * cos + q_sw * sin) * scale).astype(BF16)
    k_ref[0] = (k * cos + k_sw * sin).astype(BF16)
    vt_ref[0, 0] = v.T.astype(BF16)


def _diff_prep(pc, cos_t, sin_t):
    b, s, _ = pc.shape
    tm = ATTN_K_TILE
    return pl.pallas_call(
        functools.partial(_diff_prep_body, scale=DIFF_D ** -0.5),
        grid=(b, s // tm),
        in_specs=[pl.BlockSpec((1, tm, PC_COLS), lambda i, j: (i, j, 0)),
                  pl.BlockSpec((tm, LANES), lambda i, j: (j, 0)),
                  pl.BlockSpec((tm, LANES), lambda i, j: (j, 0))],
        out_specs=[pl.BlockSpec((1, tm, BR_WIDTH), lambda i, j: (i, j, 0)),
                   pl.BlockSpec((1, tm, BR_WIDTH), lambda i, j: (i, j, 0)),
                   pl.BlockSpec((1, 1, BR_WIDTH, tm), lambda i, j: (i, j, 0, 0))],
        out_shape=[jax.ShapeDtypeStruct((b, s, BR_WIDTH), BF16),
                   jax.ShapeDtypeStruct((b, s, BR_WIDTH), BF16),
                   jax.ShapeDtypeStruct((b, s // tm, BR_WIDTH, tm), BF16)],
        compiler_params=_cparams(("arbitrary", "arbitrary")),
        name="diff_prep",
    )(pc, cos_t, sin_t)


def _diff_attn_body(q_ref, k_ref, vt_ref, lqk_ref, g_ref, o_ref, *, n_ctx_q_tiles, n_ctx_k_tiles,
                    n_k_tiles, lam_init):
    tq = q_ref.shape[1]
    j = pl.program_id(1)
    n_k = jnp.where(j < n_ctx_q_tiles, n_ctx_k_tiles, n_k_tiles)
    lqk = lqk_ref[...]
    lam = (jnp.exp(jnp.sum(lqk[0:1] * lqk[1:2], axis=1, keepdims=True))
           - jnp.exp(jnp.sum(lqk[2:3] * lqk[3:4], axis=1, keepdims=True)) + lam_init)
    lane = _iota((tq, LANES), 1)
    dv = 2 * DIFF_D
    outs = []
    for h in range(DIFF_HEADS):
        res = []
        for mp in range(2):
            blk = h * 2 + mp
            grp = blk // 4
            sub = blk % 4
            qg = q_ref[0, :, grp * LANES:(grp + 1) * LANES]
            qm = jnp.where((lane >= sub * DIFF_D) & (lane < (sub + 1) * DIFF_D), qg, jnp.zeros_like(qg))
            acc, l = _flash_head(qm, k_ref, vt_ref, (grp * LANES, (grp + 1) * LANES),
                                 (h * dv, (h + 1) * dv), n_k, tq)
            res.append(acc / l)
        o = res[0] - lam * res[1]
        o = o * lax.rsqrt(jnp.mean(o * o, axis=0, keepdims=True) + LN_EPS)
        outs.append(o * g_ref[...] * (1.0 - lam_init))
    o_ref[0] = jnp.concatenate(outs, axis=0).T


def _diff_attn(q, k, vt, lqk, g_col, n_ctx, lam_init):
    b, s, _ = q.shape
    tq, tk = ATTN_Q_TILE, ATTN_K_TILE
    return pl.pallas_call(
        functools.partial(_diff_attn_body, n_ctx_q_tiles=n_ctx // tq, n_ctx_k_tiles=n_ctx // tk,
                          n_k_tiles=s // tk, lam_init=lam_init),
        grid=(b, s // tq),
        in_specs=[pl.BlockSpec((1, tq, BR_WIDTH), lambda i, j: (i, j, 0)),
                  pl.BlockSpec((1, s, BR_WIDTH), lambda i, j: (i, 0, 0)),
                  pl.BlockSpec((1, s // tk, BR_WIDTH, tk), lambda i, j: (i, 0, 0, 0)),
                  pl.BlockSpec(lqk.shape, lambda i, j: (0, 0)),
                  pl.BlockSpec(g_col.shape, lambda i, j: (0, 0))],
        out_specs=pl.BlockSpec((1, tq, BR_WIDTH), lambda i, j: (i, j, 0)),
        out_shape=jax.ShapeDtypeStruct((b, s, BR_WIDTH), F32),
        compiler_params=_cparams(("arbitrary", "arbitrary")),
        name="diff_attn",
    )(q, k, vt, lqk, g_col)


def _gelu_tanh(x):
    return 0.5 * x * (1.0 + jnp.tanh(math.sqrt(2.0 / math.pi) * (x + 0.044715 * (x * x * x))))


def _gmlp_body(pb_ref, g_ref, ws_ref, bias_ref, o_ref):
    z = _gelu_tanh(pb_ref[0])
    u = z[:, 0:BR_WIDTH]
    v = z[:, BR_WIDTH:2 * BR_WIDTH]
    mu = jnp.mean(v, -1, keepdims=True)
    vc = v - mu
    var = jnp.mean(vc * vc, -1, keepdims=True)
    v = (vc * lax.rsqrt(var + LN_EPS) * g_ref[...]).astype(BF16)
    lane = _iota((GMLP_CHUNK, LANES), 1)
    gw = BR_WIDTH // GMLP_GROUPS
    for c in range(pb_ref.shape[1] // GMLP_CHUNK):
        rows = slice(c * GMLP_CHUNK, (c + 1) * GMLP_CHUNK)
        for half in range(BR_WIDTH // LANES):
            cols = slice(half * LANES, (half + 1) * LANES)
            vch = v[rows, cols]
            a0 = _dot(ws_ref[2 * half], vch)
            a1 = _dot(ws_ref[2 * half + 1], vch)
            mixed = jnp.where(lane < gw, a0, a1) + bias_ref[:, cols]
            o_ref[0, rows, cols] = u[rows, cols] * mixed


def _gmlp(pb, ln_g, w_s, bias):
    b, s, _ = pb.shape
    tm = TOKEN_TILE
    full = lambda a: pl.BlockSpec(a.shape, lambda i, j: (0,) * a.ndim)
    return pl.pallas_call(
        _gmlp_body,
        grid=(b, s // tm),
        in_specs=[pl.BlockSpec((1, tm, PB_COLS), lambda i, j: (i, j, 0)),
                  full(ln_g), full(w_s), full(bias)],
        out_specs=pl.BlockSpec((1, tm, BR_WIDTH), lambda i, j: (i, j, 0)),
        out_shape=jax.ShapeDtypeStruct((b, s, BR_WIDTH), F32),
        compiler_params=_cparams(("arbitrary", "arbitrary")),
        name="gmlp",
    )(pb, ln_g, w_s, bias)


def _pair_stack(a, lane2):
    lo = jnp.where(lane2 < DN_DK, a, jnp.zeros_like(a))
    hi = jnp.where(lane2 >= DN_DK, a, jnp.zeros_like(a))
    return jnp.concatenate([lo, hi], axis=0)


def _pair_blockdiag(a, bdmask):
    return jnp.where(bdmask, jnp.concatenate([a, a], axis=0), 0.0)


def _dn_chunk_body(pd_ref, prev_ref, next_ref, pab_ref, cw_ref, alog_ref, dtb_ref,
                   u_ref, w_ref, qk_ref, qd_ref, kd_ref, el_ref,
                   qkv_s, g_s, beta_s, *, n_ctx_tiles, n_tiles):
    tm = pd_ref.shape[1]
    c = DN_CHUNK
    j = pl.program_id(1)
    x = pd_ref[0]
    has_prev = jnp.logical_and(j != 0, j != n_ctx_tiles)
    has_next = jnp.logical_and(j != n_ctx_tiles - 1, j != n_tiles - 1)
    prev_row = jnp.where(has_prev, prev_ref[0, SUBLANES - 1:SUBLANES, :], 0.0)
    next_row = jnp.where(has_next, next_ref[0, 0:1, :], 0.0)
    row = _iota((tm, 1), 0)
    x_m = jnp.where(row == 0, prev_row, pltpu.roll(x, 1, axis=0))
    x_p = jnp.where(row == tm - 1, next_row, pltpu.roll(x, tm - 1, axis=0))
    y = x_m * cw_ref[0:1, :] + x * cw_ref[1:2, :] + x_p * cw_ref[2:3, :]
    y = y * _sigmoid(y)
    ones_bd = _block_ones(BR_WIDTH, DN_DK)
    q = y[:, 0:BR_WIDTH]
    k = y[:, BR_WIDTH:2 * BR_WIDTH]
    q = q * lax.rsqrt(_dot_f32(q * q, ones_bd) + LN_EPS) * (DN_DK ** -0.5)
    k = k * lax.rsqrt(_dot_f32(k * k, ones_bd) + LN_EPS)
    qkv_s[:, 0:BR_WIDTH] = q
    qkv_s[:, BR_WIDTH:2 * BR_WIDTH] = k
    qkv_s[:, 2 * BR_WIDTH:3 * BR_WIDTH] = y[:, 2 * BR_WIDTH:3 * BR_WIDTH]
    ab = pab_ref[0]
    z = ab + dtb_ref[...]
    softplus = jnp.maximum(z, 0.0) + jnp.log(1.0 + jnp.exp(-jnp.abs(z)))
    g_s[...] = -jnp.exp(alog_ref[...]) * softplus
    beta_s[...] = _sigmoid(ab)

    ii = _iota((c, LANES), 0)
    lane2 = _iota((c, LANES), 1)
    jj = lane2 % c
    eye2 = (ii == jj).astype(F32)
    bdmask = _iota((LANES, LANES), 0) // c == _iota((LANES, LANES), 1) // c
    ri = _iota((c, c), 0)
    ci = _iota((c, c), 1)
    tri_f = (ri >= ci).astype(F32)
    tri_b = (ri <= ci).astype(F32)
    first_half = lane2 < DN_DK
    nh = DN_HEADS

    def chunk(ch, carry):
        r0 = pl.multiple_of(ch * c, c)
        rows = pl.ds(r0, c)
        gc = g_s[rows, :]
        bc = beta_s[rows, :]
        gam = jnp.where(lane2 < nh, _dot_f32(tri_f, gc), _dot_f32(tri_b, gc))
        gam_t = jnp.concatenate([gam, gam], axis=0).T
        for d in range(2):
            incl = (ii >= jj) if d == 0 else (ii <= jj)
            strict = (ii > jj) if d == 0 else (ii < jj)
            for p in range(nh // 2):
                l0 = d * nh + 2 * p
                l1 = l0 + 1
                lanes = slice(p * LANES, (p + 1) * LANES)
                gcol = jnp.where(first_half, gam[:, l0:l0 + 1], gam[:, l1:l1 + 1])
                grow = jnp.where(first_half, gam_t[l0:l0 + 1, :], gam_t[l1:l1 + 1, :])
                bcol = jnp.where(first_half, bc[:, 2 * nh + l0:2 * nh + l0 + 1],
                                 bc[:, 2 * nh + l1:2 * nh + l1 + 1])
                dec = jnp.where(incl, jnp.exp(jnp.where(incl, gcol - grow, 0.0)), 0.0)
                qg = qkv_s[rows, p * LANES:(p + 1) * LANES]
                kg = qkv_s[rows, BR_WIDTH + p * LANES:BR_WIDTH + (p + 1) * LANES]
                vg = qkv_s[rows, 2 * BR_WIDTH + p * LANES:2 * BR_WIDTH + (p + 1) * LANES]
                kb = kg * bcol
                kstack = _pair_stack(kg, lane2).astype(BF16)
                kk = _dot_nt(kb.astype(BF16), kstack)
                qk = _dot_nt(qg.astype(BF16), kstack) * dec
                pm = -jnp.where(strict, kk * dec, 0.0)
                t = eye2 + pm
                bd = _pair_blockdiag(pm, bdmask).astype(BF16)
                for _ in range(5):
                    pm = _dot(pm.astype(BF16), bd)
                    bd = _pair_blockdiag(pm, bdmask).astype(BF16)
                    t = t + _dot(t.astype(BF16), bd)
                eg = jnp.exp(gcol)
                tb = t.astype(BF16)
                u = _dot(tb, _pair_stack(vg * bcol, lane2).astype(BF16))
                w = _dot(tb, _pair_stack(kb * eg, lane2).astype(BF16))
                glast = gcol[c - 1:c, :] if d == 0 else gcol[0:1, :]
                u_ref[0, d, rows, lanes] = u
                w_ref[0, d, rows, lanes] = w
                qk_ref[0, d, rows, lanes] = qk
                qd_ref[0, d, rows, lanes] = qg * eg
                kd_ref[0, d, rows, lanes] = kg * jnp.exp(glast - gcol)
                el_ref[0, d, pl.ds(pl.multiple_of(ch * SUBLANES, SUBLANES), SUBLANES), lanes] = (
                    jnp.broadcast_to(jnp.exp(glast), (SUBLANES, LANES)))
        return carry

    lax.fori_loop(0, tm // c, chunk, 0)


def _dn_chunk(pd, pab, conv_w, alog, dtb, n_ctx):
    b, s, _ = pd.shape
    tm = TOKEN_TILE
    nt = s // tm
    nsub = tm // SUBLANES
    last_blk = s // SUBLANES - 1
    full = lambda a: pl.BlockSpec(a.shape, lambda i, j: (0,) * a.ndim)
    dir_tok = lambda: pl.BlockSpec((1, 2, tm, BR_WIDTH), lambda i, j: (i, 0, j, 0))
    dir_shape = jax.ShapeDtypeStruct((b, 2, s, BR_WIDTH), F32)
    el_rows = (tm // DN_CHUNK) * SUBLANES
    return pl.pallas_call(
        functools.partial(_dn_chunk_body, n_ctx_tiles=n_ctx // tm, n_tiles=nt),
        grid=(b, nt),
        in_specs=[pl.BlockSpec((1, tm, PD_COLS), lambda i, j: (i, j, 0)),
                  pl.BlockSpec((1, SUBLANES, PD_COLS), lambda i, j: (i, jnp.maximum(j * nsub - 1, 0), 0)),
                  pl.BlockSpec((1, SUBLANES, PD_COLS),
                               lambda i, j: (i, jnp.minimum((j + 1) * nsub, last_blk), 0)),
                  pl.BlockSpec((1, tm, PAB_COLS), lambda i, j: (i, j, 0)),
                  full(conv_w), full(alog), full(dtb)],
        out_specs=[dir_tok(), dir_tok(), dir_tok(), dir_tok(), dir_tok(),
                   pl.BlockSpec((1, 2, el_rows, BR_WIDTH), lambda i, j: (i, 0, j, 0))],
        out_shape=[dir_shape] * 5 + [jax.ShapeDtypeStruct((b, 2, nt * el_rows, BR_WIDTH), F32)],
        scratch_shapes=[pltpu.VMEM((tm, PD_COLS), F32), pltpu.VMEM((tm, PAB_COLS), F32),
                        pltpu.VMEM((tm, PAB_COLS), F32)],
        compiler_params=_cparams(("arbitrary", "arbitrary")),
        name="dn_chunk",
    )(pd, pd, pd, pab, conv_w, alog, dtb)


def _dn_scan_body(uf, wf, qkf, qdf, kdf, elf, ub, wb, qkb, qdb, kdb, elb, of_ref, ob_ref, s_ref):
    tm = uf.shape[2]
    c = DN_CHUNK
    n_chunks = tm // c

    @pl.when(pl.program_id(1) == 0)
    def _():
        s_ref[...] = jnp.zeros_like(s_ref)

    bdmask = _iota((LANES, LANES), 0) // c == _iota((LANES, LANES), 1) // c
    views = ((uf, wf, qkf, qdf, kdf, elf, of_ref), (ub, wb, qkb, qdb, kdb, elb, ob_ref))
    for step in range(n_chunks):
        for d in range(2):
            u_r, w_r, qk_r, qd_r, kd_r, el_r, o_r = views[d]
            ch = step if d == 0 else n_chunks - 1 - step
            rows = slice(ch * c, (ch + 1) * c)
            for p in range(DN_HEADS // 2):
                lanes = slice(p * LANES, (p + 1) * LANES)
                sbd = s_ref[d, p]
                sb = sbd.astype(BF16)
                vnew = u_r[0, 0, rows, lanes] - _dot(w_r[0, 0, rows, lanes].astype(BF16), sb)
                vb = vnew.astype(BF16)
                vbd = jnp.where(bdmask, jnp.concatenate([vb, vb], axis=0), jnp.zeros((), BF16))
                o_r[0, rows, lanes] = (_dot(qd_r[0, 0, rows, lanes].astype(BF16), sb)
                                       + _dot(qk_r[0, 0, rows, lanes].astype(BF16), vbd))
                upd = _dot_tn(kd_r[0, 0, rows, lanes].astype(BF16), vb)
                el = el_r[0, 0, ch * SUBLANES:ch * SUBLANES + 1, lanes]
                s_ref[d, p] = sbd * el + jnp.where(bdmask, upd, 0.0)


def _dn_scan(u, w, qk, qd, kd, el, n_ctx):
    b, _, s, _ = u.shape
    tm = TOKEN_TILE
    nt = s // tm
    nctx = n_ctx // tm
    el_rows = (tm // DN_CHUNK) * SUBLANES

    def rev(j):
        return jnp.where(j < nctx, nctx - 1 - j, nt - 1 - (j - nctx))

    fwd = lambda rows: pl.BlockSpec((1, 1, rows, BR_WIDTH), lambda i, j: (i, 0, j, 0))
    bwd = lambda rows: pl.BlockSpec((1, 1, rows, BR_WIDTH), lambda i, j: (i, 1, rev(j), 0))
    in_specs = [fwd(tm)] * 5 + [fwd(el_rows)] + [bwd(tm)] * 5 + [bwd(el_rows)]
    out_shape = jax.ShapeDtypeStruct((b, s, BR_WIDTH), F32)
    return pl.pallas_call(
        _dn_scan_body,
        grid=(b, nt),
        in_specs=in_specs,
        out_specs=[pl.BlockSpec((1, tm, BR_WIDTH), lambda i, j: (i, j, 0)),
                   pl.BlockSpec((1, tm, BR_WIDTH), lambda i, j: (i, rev(j), 0))],
        out_shape=[out_shape, out_shape],
        scratch_shapes=[pltpu.VMEM((2, DN_HEADS // 2, LANES, LANES), F32)],
        compiler_params=_cparams(("arbitrary", "arbitrary")),
        name="dn_scan",
    )(u, w, qk, qd, kd, el, u, w, qk, qd, kd, el)


def _merge_body(x_ref, mod_ref, h_ref, pg_ref, ya_ref, yb_ref, yc_ref, of_ref, ob_ref, dng_ref,
                wg_ref, wb_ref, wo_ref, lng_ref, lnb_ref, o_ref, *, alpha):
    hb = h_ref[0]
    o = of_ref[0] + ob_ref[0]
    ms = _dot_f32(o * o, _block_ones(BR_WIDTH, DN_DV)) * (1.0 / DN_DV)
    yd = o * lax.rsqrt(ms + LN_EPS) * dng_ref[...]
    pg = pg_ref[0]
    sg = pg * _sigmoid(pg)
    acc = None
    for i, y in enumerate((ya_ref[0], yb_ref[0], yc_ref[0], yd)):
        t = (y * sg[:, i * BR_WIDTH:(i + 1) * BR_WIDTH]).astype(BF16)
        term = _sigmoid(_dot(hb, wg_ref[i])) * _dot(t, wb_ref[i])
        acc = term if acc is None else acc + term
    out = _dot(acc.astype(BF16), wo_ref[...])
    gt = mod_ref[0, 0, 2:3, :]
    r = alpha * x_ref[0] + gt * out
    mu = jnp.mean(r, -1, keepdims=True)
    rc = r - mu
    var = jnp.mean(rc * rc, -1, keepdims=True)
    o_ref[0] = rc * lax.rsqrt(var + LN_EPS) * lng_ref[...] + lnb_ref[...]


def _merge(xs, mod, h, pg, ya, yb, yc, o_f, o_b, dng, wg, wb, wo, lng, lnb, n_ctx_tiles, skip_tiles, alpha):
    b, s, d = xs.shape
    tm = TOKEN_TILE
    nt = s // tm - skip_tiles
    tok = lambda width: pl.BlockSpec((1, tm, width), lambda i, j: (i, j + skip_tiles, 0))
    full = lambda a: pl.BlockSpec(a.shape, lambda i, j: (0,) * a.ndim)
    return pl.pallas_call(
        functools.partial(_merge_body, alpha=alpha),
        grid=(b, nt),
        in_specs=[tok(d),
                  pl.BlockSpec((1, 1, 3, d),
                               lambda i, j: (i, jnp.where(j + skip_tiles >= n_ctx_tiles, 1, 0), 0, 0)),
                  tok(d), tok(PG_COLS), tok(BR_WIDTH), tok(BR_WIDTH), tok(BR_WIDTH), tok(BR_WIDTH),
                  tok(BR_WIDTH), full(dng), full(wg), full(wb), full(wo), full(lng), full(lnb)],
        out_specs=pl.BlockSpec((1, tm, d), lambda i, j: (i, j, 0)),
        out_shape=jax.ShapeDtypeStruct((b, nt * tm, d), F32),
        compiler_params=_cparams(("arbitrary", "arbitrary")),
        name="merge",
    )(xs, mod, h, pg, ya, yb, yc, o_f, o_b, dng, wg, wb, wo, lng, lnb)


def _rope_swap_index(n_blocks):
    blk = jnp.array(list(range(8, 16)) + list(range(0, 8)) + list(range(24, 32)) + list(range(16, 24)))
    return (jnp.arange(n_blocks)[:, None] * 32 + blk[None, :]).reshape(-1)


def _pack_w_in(w):
    d = w.shape[0]
    o = 0
    cq = w[:, o:o + MLA_Q_LORA]; o += MLA_Q_LORA
    ckv = w[:, o:o + MLA_KV_LORA]; o += MLA_KV_LORA
    kr = w[:, o:o + MLA_ROPE]; o += MLA_ROPE
    pb = w[:, o:o + GMLP_COLS]; o += GMLP_COLS
    dq = w[:, o:o + 256]; dk = w[:, o + 256:o + 512]; dv = w[:, o + 512:o + 768]; o += DIFF_COLS
    dn_qkv = w[:, o:o + 3 * BR_WIDTH]; dn_ab = w[:, o + 3 * BR_WIDTH:o + DN_COLS]; o += DN_COLS
    pg = w[:, o:]
    z = lambda n: jnp.zeros((d, n), w.dtype)
    sw1 = _rope_swap_index(1)
    sw8 = _rope_swap_index(8)
    place = lambda a: jnp.concatenate([z(MLA_NOPE), a, z(LANES - MLA_NOPE - MLA_ROPE)], axis=1)
    packed = jnp.concatenate(
        [cq, ckv, place(kr), place(kr[:, sw1]), pb, dq, dk, dv, dq[:, sw8], dk[:, sw8], dn_qkv,
         dn_ab, z(PAB_COLS - 4 * DN_HEADS), pg], axis=1)
    return packed.astype(BF16)


def _pack_mla_weights(w_uq, w_ukv):
    dq = MLA_NOPE + MLA_ROPE
    wq = w_uq.reshape(MLA_Q_LORA, MLA_HEADS, dq)
    zq = jnp.zeros((MLA_Q_LORA, MLA_HEADS, LANES - dq), w_uq.dtype)
    wq_p = jnp.concatenate([wq, zq], axis=2).reshape(MLA_Q_LORA, MLA_HEADS * LANES)
    rope_sw = wq[:, :, MLA_NOPE:][:, :, _rope_swap_index(1)]
    wqs_p = jnp.concatenate([jnp.zeros((MLA_Q_LORA, MLA_HEADS, MLA_NOPE), w_uq.dtype), rope_sw, zq],
                            axis=2).reshape(MLA_Q_LORA, MLA_HEADS * LANES)
    wkv = w_ukv.reshape(MLA_KV_LORA, MLA_HEADS, MLA_NOPE + MLA_V)
    zk = jnp.zeros((MLA_KV_LORA, MLA_HEADS, LANES - MLA_NOPE), w_ukv.dtype)
    wk_p = jnp.concatenate([wkv[:, :, :MLA_NOPE], zk], axis=2).reshape(MLA_KV_LORA, MLA_HEADS * LANES)
    wv_p = wkv[:, :, MLA_NOPE:].reshape(MLA_KV_LORA, MLA_HEADS * MLA_V)
    return wq_p.astype(BF16), wqs_p.astype(BF16), wk_p.astype(BF16), wv_p.astype(BF16)


def _rope_tables(n, n_ctx):
    rows = n // GRID_W
    row = jnp.repeat(jnp.arange(rows, dtype=F32), GRID_W)
    col = jnp.tile(jnp.arange(GRID_W, dtype=F32), rows)
    axis_dim = MLA_ROPE // 2
    inv_freq = ROPE_BASE ** (-jnp.arange(0, axis_dim, 2, dtype=F32) / axis_dim)
    ar = row[:, None] * inv_freq
    ac = col[:, None] * inv_freq
    cos32 = jnp.concatenate([jnp.cos(ar), jnp.cos(ar), jnp.cos(ac), jnp.cos(ac)], axis=1)
    sin32 = jnp.concatenate([-jnp.sin(ar), jnp.sin(ar), -jnp.sin(ac), jnp.sin(ac)], axis=1)
    cos32 = jnp.concatenate([jnp.ones((n_ctx, 32), F32), cos32], axis=0)
    sin32 = jnp.concatenate([jnp.zeros((n_ctx, 32), F32), sin32], axis=0)
    s = n + n_ctx
    one = lambda w: jnp.ones((s, w), F32)
    zero = lambda w: jnp.zeros((s, w), F32)
    mla_cos = jnp.concatenate([one(MLA_NOPE), cos32, one(LANES - MLA_NOPE - MLA_ROPE)], axis=1)
    mla_sin = jnp.concatenate([zero(MLA_NOPE), sin32, zero(LANES - MLA_NOPE - MLA_ROPE)], axis=1)
    diff_cos = jnp.tile(cos32, (1, LANES // 32))
    diff_sin = jnp.tile(sin32, (1, LANES // 32))
    return mla_cos, mla_sin, diff_cos, diff_sin


def _pad_lanes(a, width=LANES):
    return jnp.concatenate([a, jnp.zeros(a.shape[:-1] + (width - a.shape[-1],), a.dtype)], axis=-1)


def kernel(x, c, ctx, c_ctx, w_mod, b_mod, w_in, mla_q_norm, mla_w_uq, mla_kv_norm, mla_w_ukv, gmlp_ln_g, gmlp_w_s, gmlp_b_s, diff_lq1, diff_lk1, diff_lq2, diff_lk2, diff_norm_g, dn_conv_w, dn_a_log, dn_dt_bias, dn_norm_g, w_gate, w_branch, w_out, ln_g, ln_b):
    b, n, d = x.shape
    n_ctx = ctx.shape[1]
    depth = w_mod.shape[0]
    tm = TOKEN_TILE
    assert d == D_MODEL and n % tm == 0 and n_ctx % tm == 0 and n % GRID_W == 0
    assert tm == ATTN_Q_TILE == ATTN_K_TILE
    alpha = (2 * depth) ** 0.25
    n_ctx_tiles = n_ctx // tm

    xs = jnp.concatenate([ctx, x], axis=1)
    mla_cos, mla_sin, diff_cos, diff_sin = _rope_tables(n, n_ctx)

    rows = ((b + 1 + SUBLANES - 1) // SUBLANES) * SUBLANES
    cc = jnp.concatenate([c, c_ctx[None, :], jnp.zeros((rows - b - 1, d), F32)], axis=0)
    mod_all = _modulation(cc, w_mod, b_mod)

    for l in range(depth):
        last = l == depth - 1
        lam_init = 0.8 - 0.6 * math.exp(-0.3 * l)
        mod_l = mod_all[l].reshape(rows, 3, d)
        mod = jnp.stack([jnp.broadcast_to(mod_l[b][None], (b, 3, d)), mod_l[:b]], axis=1)

        h, pa, pb, pc, pd, pab, pg = _inproj(xs, mod, _pack_w_in(w_in[l]), n_ctx_tiles)

        wq, wqs, wk, wv = _pack_mla_weights(mla_w_uq[l], mla_w_ukv[l])
        q_a, k_a, vt_a = _mla_prep(pa, mla_cos, mla_sin, mla_q_norm[l][None, :], mla_kv_norm[l][None, :],
                                   wq, wqs, wk, wv)
        ya = _mla_attn(q_a, k_a, vt_a, n_ctx)

        bias = jnp.repeat(gmlp_b_s[l].T, BR_WIDTH // GMLP_GROUPS, axis=1)
        yb = _gmlp(pb, gmlp_ln_g[l][None, :], gmlp_w_s[l].astype(BF16), bias)

        q_c, k_c, vt_c = _diff_prep(pc, diff_cos, diff_sin)
        lqk = _pad_lanes(jnp.stack([diff_lq1[l], diff_lk1[l], diff_lq2[l], diff_lk2[l]], axis=0))
        yc = _diff_attn(q_c, k_c, vt_c, lqk, diff_norm_g[l][:, None], n_ctx, lam_init)

        conv_w = jnp.concatenate([dn_conv_w[l], jnp.zeros((SUBLANES - 3, PD_COLS), F32)], axis=0)
        alog = _pad_lanes(dn_a_log[l].reshape(1, 2 * DN_HEADS))
        dtb = _pad_lanes(dn_dt_bias[l].reshape(1, 2 * DN_HEADS))
        u, w, qk, qd, kd, el = _dn_chunk(pd, pab, conv_w, alog, dtb, n_ctx)
        o_f, o_b = _dn_scan(u, w, qk, qd, kd, el, n_ctx)

        dng = jnp.tile(dn_norm_g[l], DN_HEADS)[None, :]
        xs = _merge(xs, mod, h, pg, ya, yb, yc, o_f, o_b, dng, w_gate[l].astype(BF16),
                    w_branch[l].astype(BF16), w_out[l].astype(BF16), ln_g[l][None, :], ln_b[l][None, :],
                    n_ctx_tiles, n_ctx_tiles if last else 0, alpha)
    return xs
```

```python
import functools
import math

import jax
import jax.numpy as jnp
from jax import lax
from jax.experimental import pallas as pl
from jax.experimental.pallas import tpu as pltpu

F32 = jnp.float32
BF16 = jnp.bfloat16

D_MODEL = 1024
GRID_W = 64
N_BRANCH = 4
BR_WIDTH = 256
MLA_HEADS = 4
MLA_NOPE = 64
MLA_ROPE = 32
MLA_V = 64
MLA_Q_LORA = 256
MLA_KV_LORA = 128
GMLP_GROUPS = 4
GMLP_CHUNK = 128
DIFF_HEADS = 4
DIFF_D = 32
DN_HEADS = 4
DN_DK = 64
DN_DV = 64
DN_CHUNK = 64
ROPE_BASE = 10000.0
LN_EPS = 1e-6

MLA_COLS = MLA_Q_LORA + MLA_KV_LORA + MLA_ROPE
GMLP_COLS = 2 * BR_WIDTH
DIFF_COLS = 3 * DIFF_HEADS * 2 * DIFF_D
DN_COLS = 3 * BR_WIDTH + 4 * DN_HEADS

LANES = 128
SUBLANES = 8
VMEM_LIMIT_BYTES = 56 * 1024 * 1024

TOKEN_TILE = 256
ATTN_Q_TILE = 256
ATTN_K_TILE = 256
ATTN_LOOKAHEAD = 3
NEG_BIG = -1e30
VT_ROWS = 80
LOG2_E = 1.4426950408889634

PA_COLS = 640
PB_COLS = 512
PC_COLS = 1280
PD_COLS = 768
PAB_COLS = 128
PG_COLS = 1024
PACK_SPLITS = (PA_COLS, PB_COLS, PC_COLS, PD_COLS, PAB_COLS, PG_COLS)
PACK_COLS = sum(PACK_SPLITS)


def _cparams(semantics):
    return pltpu.CompilerParams(dimension_semantics=semantics, vmem_limit_bytes=VMEM_LIMIT_BYTES)


def _dot(a, b):
    return jnp.dot(a, b, preferred_element_type=F32)


def _dot_nt(a, b):
    return lax.dot_general(a, b, (((1,), (1,)), ((), ())), preferred_element_type=F32)


def _dot_tn(a, b):
    return lax.dot_general(a, b, (((0,), (0,)), ((), ())), preferred_element_type=F32)


def _dot_f32(a, b):
    return jnp.dot(a, b, preferred_element_type=F32, precision=lax.Precision.HIGHEST)


def _sigmoid(x):
    return 1.0 / (1.0 + jnp.exp(-x))


def _iota(shape, axis):
    return lax.broadcasted_iota(jnp.int32, shape, axis)


def _block_ones(n, blk):
    return (_iota((n, n), 0) // blk == _iota((n, n), 1) // blk).astype(F32)


def _mod_body(c_ref, w_ref, b_ref, o_ref):
    c = c_ref[...]
    s = (c * _sigmoid(c)).astype(BF16)
    o_ref[0] = _dot(s, w_ref[0].astype(BF16)) + b_ref[0]


def _modulation(cc, w_mod, b_mod):
    depth, d, n3 = w_mod.shape
    rows = cc.shape[0]
    tn = 1024
    return pl.pallas_call(
        _mod_body,
        grid=(depth, n3 // tn),
        in_specs=[pl.BlockSpec((rows, d), lambda l, j: (0, 0)),
                  pl.BlockSpec((1, d, tn), lambda l, j: (l, 0, j)),
                  pl.BlockSpec((1, 1, tn), lambda l, j: (l, 0, j))],
        out_specs=pl.BlockSpec((1, rows, tn), lambda l, j: (l, 0, j)),
        out_shape=jax.ShapeDtypeStruct((depth, rows, n3), F32),
        compiler_params=_cparams(("arbitrary", "arbitrary")),
        name="modulation",
    )(cc, w_mod, b_mod.reshape(depth, 1, n3))


def _inproj_body(x_ref, mod_ref, w_ref, h_ref, *out_refs):
    x = x_ref[0]
    mu = jnp.mean(x, -1, keepdims=True)
    xc = x - mu
    var = jnp.mean(xc * xc, -1, keepdims=True)
    xn = xc * lax.rsqrt(var + LN_EPS)
    sh = mod_ref[0, 0, 0:1, :]
    sc = mod_ref[0, 0, 1:2, :]
    hb = (xn * (1.0 + sc) + sh).astype(BF16)
    h_ref[0] = hb
    off = 0
    for ref, width in zip(out_refs, PACK_SPLITS):
        ref[0] = _dot(hb, w_ref[:, off:off + width])
        off += width


def _inproj(xs, mod, w_pack, n_ctx_tiles):
    b, s, d = xs.shape
    tm = TOKEN_TILE
    tok = lambda width, dt: jax.ShapeDtypeStruct((b, s, width), dt)
    tok_spec = lambda width: pl.BlockSpec((1, tm, width), lambda i, j: (i, j, 0))
    return pl.pallas_call(
        _inproj_body,
        grid=(b, s // tm),
        in_specs=[tok_spec(d),
                  pl.BlockSpec((1, 1, 3, d), lambda i, j: (i, jnp.where(j >= n_ctx_tiles, 1, 0), 0, 0)),
                  pl.BlockSpec((d, PACK_COLS), lambda i, j: (0, 0))],
        out_specs=[tok_spec(d)] + [tok_spec(w) for w in PACK_SPLITS],
        out_shape=[tok(d, BF16)] + [tok(w, F32) for w in PACK_SPLITS],
        compiler_params=_cparams(("arbitrary", "arbitrary")),
        name="inproj",
    )(xs, mod, w_pack)


def _rms(x, g):
    return x * lax.rsqrt(jnp.mean(x * x, -1, keepdims=True) + LN_EPS) * g


def _mla_prep_body(pa_ref, cos_ref, sin_ref, qn_ref, kvn_ref, wq_ref, wqs_ref, wk_ref, wv_ref,
                   q_ref, k_ref, vt_ref, *, scale):
    pa = pa_ref[0]
    cq = pa[:, 0:MLA_Q_LORA]
    ckv = pa[:, MLA_Q_LORA:MLA_Q_LORA + MLA_KV_LORA]
    kr = pa[:, 384:512]
    kr_sw = pa[:, 512:640]
    cos = cos_ref[...]
    sin = sin_ref[...]
    cos4 = jnp.concatenate([cos] * MLA_HEADS, axis=1)
    sin4 = jnp.concatenate([sin] * MLA_HEADS, axis=1)
    cqn = _rms(cq, qn_ref[...]).astype(BF16)
    q = _dot(cqn, wq_ref[...])
    q_sw = _dot(cqn, wqs_ref[...])
    q_ref[0] = ((q * cos4 + q_sw * sin4) * scale).astype(BF16)
    ckvn = _rms(ckv, kvn_ref[...]).astype(BF16)
    kn = _dot(ckvn, wk_ref[...])
    kr_rot = kr * cos + kr_sw * sin
    k_ref[0] = (kn + jnp.concatenate([kr_rot] * MLA_HEADS, axis=1)).astype(BF16)
    v = _dot(ckvn, wv_ref[...])
    vt_ref[0, 0] = _vt_with_ones(v, MLA_HEADS, MLA_V)


def _mla_prep(pa, cos_t, sin_t, qn, kvn, wq, wqs, wk, wv):
    b, s, _ = pa.shape
    tm = ATTN_K_TILE
    hp = MLA_HEADS * LANES
    full = lambda a: pl.BlockSpec(a.shape, lambda i, j: (0,) * a.ndim)
    scale = (MLA_NOPE + MLA_ROPE) ** -0.5 * LOG2_E
    return pl.pallas_call(
        functools.partial(_mla_prep_body, scale=scale),
        grid=(b, s // tm),
        in_specs=[pl.BlockSpec((1, tm, PA_COLS), lambda i, j: (i, j, 0)),
                  pl.BlockSpec((tm, LANES), lambda i, j: (j, 0)),
                  pl.BlockSpec((tm, LANES), lambda i, j: (j, 0)),
                  full(qn), full(kvn), full(wq), full(wqs), full(wk), full(wv)],
        out_specs=[pl.BlockSpec((1, tm, hp), lambda i, j: (i, j, 0)),
                   pl.BlockSpec((1, tm, hp), lambda i, j: (i, j, 0)),
                   pl.BlockSpec((1, 1, 4 * VT_ROWS, tm), lambda i, j: (i, j, 0, 0))],
        out_shape=[jax.ShapeDtypeStruct((b, s, hp), BF16),
                   jax.ShapeDtypeStruct((b, s, hp), BF16),
                   jax.ShapeDtypeStruct((b, s // tm, 4 * VT_ROWS, tm), BF16)],
        compiler_params=_cparams(("arbitrary", "arbitrary")),
        name="mla_prep",
    )(pa, cos_t, sin_t, qn, kvn, wq, wqs, wk, wv)


def _vt_with_ones(v, heads, dv):
    tm = v.shape[0]
    vt = v.T
    aug = (_iota((VT_ROWS - dv, tm), 0) == 0).astype(F32)
    pieces = []
    for h in range(heads):
        pieces += [vt[h * dv:(h + 1) * dv], aug]
    return jnp.concatenate(pieces, axis=0).astype(BF16)


def _flash_chains(q_of, k_ref, vt_ref, k_lanes, v_rows, n_k, m_s, acc_s):
    tk = ATTN_K_TILE
    m_s[...] = jnp.full(m_s.shape, NEG_BIG, F32)
    acc_s[...] = jnp.zeros(acc_s.shape, F32)

    n_chain = len(k_lanes)

    def scores(c, kt):
        k_rows = pl.ds(pl.multiple_of(kt * tk, tk), tk)
        return _dot_nt(k_ref[0, k_rows, k_lanes[c][0]:k_lanes[c][1]], q_of(c))

    def step(kt, ahead):
        ahead = list(ahead)
        kt_ahead = jnp.minimum(kt + 1, n_k - 1)
        for c in range(n_chain):
            s = ahead.pop(0)
            nxt = c + ATTN_LOOKAHEAD
            ahead.append(scores(nxt, kt) if nxt < n_chain else scores(nxt - n_chain, kt_ahead))
            m_old = m_s[c]
            m_new = jnp.maximum(m_old, jnp.max(s, axis=0, keepdims=True))
            p = jnp.exp2(s - m_new).astype(BF16)
            vt = vt_ref[0, kt, v_rows[c][0]:v_rows[c][1], :]
            acc_s[c] = jnp.exp2(m_old - m_new) * acc_s[c] + _dot(vt, p)
            m_s[c] = m_new
        return tuple(ahead)

    lax.fori_loop(0, n_k, step, tuple(scores(c, 0) for c in range(ATTN_LOOKAHEAD)))


def _mla_attn_body(q_ref, k_ref, vt_ref, o_ref, m_s, acc_s, *, n_ctx_q_tiles, n_ctx_k_tiles, n_k_tiles):
    j = pl.program_id(1)
    n_k = jnp.where(j < n_ctx_q_tiles, n_ctx_k_tiles, n_k_tiles)
    k_lanes = [(h * LANES, (h + 1) * LANES) for h in range(MLA_HEADS)]
    v_rows = [(h * VT_ROWS, (h + 1) * VT_ROWS) for h in range(MLA_HEADS)]
    _flash_chains(lambda c: q_ref[0, :, c * LANES:(c + 1) * LANES], k_ref, vt_ref, k_lanes, v_rows,
                  n_k, m_s, acc_s)
    outs = []
    for h in range(MLA_HEADS):
        acc = acc_s[h]
        outs.append(acc[0:MLA_V] / acc[MLA_V:MLA_V + 1])
    o_ref[0] = jnp.concatenate(outs, axis=0).T


def _mla_attn(q, k, vt, n_ctx):
    b, s, hp = q.shape
    tq, tk = ATTN_Q_TILE, ATTN_K_TILE
    return pl.pallas_call(
        functools.partial(_mla_attn_body, n_ctx_q_tiles=n_ctx // tq, n_ctx_k_tiles=n_ctx // tk,
                          n_k_tiles=s // tk),
        grid=(b, s // tq),
        in_specs=[pl.BlockSpec((1, tq, hp), lambda i, j: (i, j, 0)),
                  pl.BlockSpec((1, s, hp), lambda i, j: (i, 0, 0)),
                  pl.BlockSpec((1, s // tk, MLA_HEADS * VT_ROWS, tk), lambda i, j: (i, 0, 0, 0))],
        out_specs=pl.BlockSpec((1, tq, BR_WIDTH), lambda i, j: (i, j, 0)),
        out_shape=jax.ShapeDtypeStruct((b, s, BR_WIDTH), F32),
        scratch_shapes=[pltpu.VMEM((MLA_HEADS, 1, tq), F32), pltpu.VMEM((MLA_HEADS, VT_ROWS, tq), F32)],
        compiler_params=_cparams(("arbitrary", "arbitrary")),
        name="mla_attn",
    )(q, k, vt)


def _diff_prep_body(pc_ref, cos_ref, sin_ref, q_ref, k_ref, vt_ref, *, scale):
    pc = pc_ref[0]
    cos = jnp.concatenate([cos_ref[...]] * 2, axis=1)
    sin = jnp.concatenate([sin_ref[...]] * 2, axis=1)
    q = pc[:, 0:256]
    k = pc[:, 256:512]
    v = pc[:, 512:768]
    q_sw = pc[:, 768:1024]
    k_sw = pc[:, 1024:1280]
    q_ref[0] = ((q * cos + q_sw * sin) * scale).astype(BF16)
    k_ref[0] = (k * cos + k_sw * sin).astype(BF16)
    vt_ref[0, 0] = _vt_with_ones(v, DIFF_HEADS, 2 * DIFF_D)


def _diff_prep(pc, cos_t, sin_t):
    b, s, _ = pc.shape
    tm = ATTN_K_TILE
    return pl.pallas_call(
        functools.partial(_diff_prep_body, scale=DIFF_D ** -0.5 * LOG2_E),
        grid=(b, s // tm),
        in_specs=[pl.BlockSpec((1, tm, PC_COLS), lambda i, j: (i, j, 0)),
                  pl.BlockSpec((tm, LANES), lambda i, j: (j, 0)),
                  pl.BlockSpec((tm, LANES), lambda i, j: (j, 0))],
        out_specs=[pl.BlockSpec((1, tm, BR_WIDTH), lambda i, j: (i, j, 0)),
                   pl.BlockSpec((1, tm, BR_WIDTH), lambda i, j: (i, j, 0)),
                   pl.BlockSpec((1, 1, 4 * VT_ROWS, tm), lambda i, j: (i, j, 0, 0))],
        out_shape=[jax.ShapeDtypeStruct((b, s, BR_WIDTH), BF16),
                   jax.ShapeDtypeStruct((b, s, BR_WIDTH), BF16),
                   jax.ShapeDtypeStruct((b, s // tm, 4 * VT_ROWS, tm), BF16)],
        compiler_params=_cparams(("arbitrary", "arbitrary")),
        name="diff_prep",
    )(pc, cos_t, sin_t)


def _diff_attn_body(q_ref, k_ref, vt_ref, lqk_ref, g_ref, o_ref, qm_s, m_s, acc_s, *, n_ctx_q_tiles,
                    n_ctx_k_tiles, n_k_tiles, lam_init):
    tq = q_ref.shape[1]
    j = pl.program_id(1)
    n_k = jnp.where(j < n_ctx_q_tiles, n_ctx_k_tiles, n_k_tiles)
    lqk = lqk_ref[...]
    lam = (jnp.exp(jnp.sum(lqk[0:1] * lqk[1:2], axis=1, keepdims=True))
           - jnp.exp(jnp.sum(lqk[2:3] * lqk[3:4], axis=1, keepdims=True)) + lam_init)
    lane = _iota((tq, LANES), 1)
    dv = 2 * DIFF_D
    k_lanes, v_rows = [], []
    for blk in range(2 * DIFF_HEADS):
        grp, sub = blk // 4, blk % 4
        qg = q_ref[0, :, grp * LANES:(grp + 1) * LANES]
        qm_s[blk] = jnp.where((lane >= sub * DIFF_D) & (lane < (sub + 1) * DIFF_D), qg, jnp.zeros_like(qg))
        k_lanes.append((grp * LANES, (grp + 1) * LANES))
        v_rows.append(((blk // 2) * VT_ROWS, (blk // 2 + 1) * VT_ROWS))
    _flash_chains(lambda c: qm_s[c], k_ref, vt_ref, k_lanes, v_rows, n_k, m_s, acc_s)
    outs = []
    for h in range(DIFF_HEADS):
        a1 = acc_s[2 * h]
        a2 = acc_s[2 * h + 1]
        o = a1[0:dv] / a1[dv:dv + 1] - lam * (a2[0:dv] / a2[dv:dv + 1])
        o = o * lax.rsqrt(jnp.mean(o * o, axis=0, keepdims=True) + LN_EPS)
        outs.append(o * g_ref[...] * (1.0 - lam_init))
    o_ref[0] = jnp.concatenate(outs, axis=0).T


def _diff_attn(q, k, vt, lqk, g_col, n_ctx, lam_init):
    b, s, _ = q.shape
    tq, tk = ATTN_Q_TILE, ATTN_K_TILE
    return pl.pallas_call(
        functools.partial(_diff_attn_body, n_ctx_q_tiles=n_ctx // tq, n_ctx_k_tiles=n_ctx // tk,
                          n_k_tiles=s // tk, lam_init=lam_init),
        grid=(b, s // tq),
        in_specs=[pl.BlockSpec((1, tq, BR_WIDTH), lambda i, j: (i, j, 0)),
                  pl.BlockSpec((1, s, BR_WIDTH), lambda i, j: (i, 0, 0)),
                  pl.BlockSpec((1, s // tk, DIFF_HEADS * VT_ROWS, tk), lambda i, j: (i, 0, 0, 0)),
                  pl.BlockSpec(lqk.shape, lambda i, j: (0, 0)),
                  pl.BlockSpec(g_col.shape, lambda i, j: (0, 0))],
        out_specs=pl.BlockSpec((1, tq, BR_WIDTH), lambda i, j: (i, j, 0)),
        out_shape=jax.ShapeDtypeStruct((b, s, BR_WIDTH), F32),
        scratch_shapes=[pltpu.VMEM((2 * DIFF_HEADS, tq, LANES), BF16),
                        pltpu.VMEM((2 * DIFF_HEADS, 1, tq), F32),
                        pltpu.VMEM((2 * DIFF_HEADS, VT_ROWS, tq), F32)],
        compiler_params=_cparams(("arbitrary", "arbitrary")),
        name="diff_attn",
    )(q, k, vt, lqk, g_col)


def _gelu_tanh(x):
    return 0.5 * x * (1.0 + jnp.tanh(math.sqrt(2.0 / math.pi) * (x + 0.044715 * (x * x * x))))


def _gmlp_body(pb_ref, g_ref, ws_ref, bias_ref, o_ref):
    z = _gelu_tanh(pb_ref[0])
    u = z[:, 0:BR_WIDTH]
    v = z[:, BR_WIDTH:2 * BR_WIDTH]
    mu = jnp.mean(v, -1, keepdims=True)
    vc = v - mu
    var = jnp.mean(vc * vc, -1, keepdims=True)
    v = (vc * lax.rsqrt(var + LN_EPS) * g_ref[...]).astype(BF16)
    lane = _iota((GMLP_CHUNK, LANES), 1)
    gw = BR_WIDTH // GMLP_GROUPS
    for c in range(pb_ref.shape[1] // GMLP_CHUNK):
        rows = slice(c * GMLP_CHUNK, (c + 1) * GMLP_CHUNK)
        for half in range(BR_WIDTH // LANES):
            cols = slice(half * LANES, (half + 1) * LANES)
            vch = v[rows, cols]
            a0 = _dot(ws_ref[2 * half], vch)
            a1 = _dot(ws_ref[2 * half + 1], vch)
            mixed = jnp.where(lane < gw, a0, a1) + bias_ref[:, cols]
            o_ref[0, rows, cols] = u[rows, cols] * mixed


def _gmlp(pb, ln_g, w_s, bias):
    b, s, _ = pb.shape
    tm = TOKEN_TILE
    full = lambda a: pl.BlockSpec(a.shape, lambda i, j: (0,) * a.ndim)
    return pl.pallas_call(
        _gmlp_body,
        grid=(b, s // tm),
        in_specs=[pl.BlockSpec((1, tm, PB_COLS), lambda i, j: (i, j, 0)),
                  full(ln_g), full(w_s), full(bias)],
        out_specs=pl.BlockSpec((1, tm, BR_WIDTH), lambda i, j: (i, j, 0)),
        out_shape=jax.ShapeDtypeStruct((b, s, BR_WIDTH), F32),
        compiler_params=_cparams(("arbitrary", "arbitrary")),
        name="gmlp",
    )(pb, ln_g, w_s, bias)


def _pair_stack(a, lane2):
    lo = jnp.where(lane2 < DN_DK, a, jnp.zeros_like(a))
    hi = jnp.where(lane2 >= DN_DK, a, jnp.zeros_like(a))
    return jnp.concatenate([lo, hi], axis=0)


def _pair_blockdiag(a, bdmask):
    return jnp.where(bdmask, jnp.concatenate([a, a], axis=0), 0.0)


def _dn_chunk_body(pd_ref, prev_ref, next_ref, pab_ref, cw_ref, alog_ref, dtb_ref,
                   u_ref, w_ref, qk_ref, qd_ref, kd_ref, el_ref,
                   qkv_s, g_s, beta_s, *, n_ctx_tiles, n_tiles):
    tm = pd_ref.shape[1]
    c = DN_CHUNK
    j = pl.program_id(1)
    x = pd_ref[0]
    has_prev = jnp.logical_and(j != 0, j != n_ctx_tiles)
    has_next = jnp.logical_and(j != n_ctx_tiles - 1, j != n_tiles - 1)
    prev_row = jnp.where(has_prev, prev_ref[0, SUBLANES - 1:SUBLANES, :], 0.0)
    next_row = jnp.where(has_next, next_ref[0, 0:1, :], 0.0)
    row = _iota((tm, 1), 0)
    x_m = jnp.where(row == 0, prev_row, pltpu.roll(x, 1, axis=0))
    x_p = jnp.where(row == tm - 1, next_row, pltpu.roll(x, tm - 1, axis=0))
    y = x_m * cw_ref[0:1, :] + x * cw_ref[1:2, :] + x_p * cw_ref[2:3, :]
    y = y * _sigmoid(y)
    ones_bd = _block_ones(BR_WIDTH, DN_DK)
    q = y[:, 0:BR_WIDTH]
    k = y[:, BR_WIDTH:2 * BR_WIDTH]
    q = q * lax.rsqrt(_dot_f32(q * q, ones_bd) + LN_EPS) * (DN_DK ** -0.5)
    k = k * lax.rsqrt(_dot_f32(k * k, ones_bd) + LN_EPS)
    qkv_s[:, 0:BR_WIDTH] = q
    qkv_s[:, BR_WIDTH:2 * BR_WIDTH] = k
    qkv_s[:, 2 * BR_WIDTH:3 * BR_WIDTH] = y[:, 2 * BR_WIDTH:3 * BR_WIDTH]
    ab = pab_ref[0]
    z = ab + dtb_ref[...]
    softplus = jnp.maximum(z, 0.0) + jnp.log(1.0 + jnp.exp(-jnp.abs(z)))
    g_s[...] = -jnp.exp(alog_ref[...]) * softplus
    beta_s[...] = _sigmoid(ab)

    ii = _iota((c, LANES), 0)
    lane2 = _iota((c, LANES), 1)
    jj = lane2 % c
    eye2 = (ii == jj).astype(F32)
    bdmask = _iota((LANES, LANES), 0) // c == _iota((LANES, LANES), 1) // c
    ri = _iota((c, c), 0)
    ci = _iota((c, c), 1)
    tri_f = (ri >= ci).astype(F32)
    tri_b = (ri <= ci).astype(F32)
    first_half = lane2 < DN_DK
    nh = DN_HEADS

    def chunk(ch, carry):
        r0 = pl.multiple_of(ch * c, c)
        rows = pl.ds(r0, c)
        gc = g_s[rows, :]
        bc = beta_s[rows, :]
        gam = jnp.where(lane2 < nh, _dot_f32(tri_f, gc), _dot_f32(tri_b, gc))
        gam_t = jnp.concatenate([gam, gam], axis=0).T
        for d in range(2):
            incl = (ii >= jj) if d == 0 else (ii <= jj)
            strict = (ii > jj) if d == 0 else (ii < jj)
            for p in range(nh // 2):
                l0 = d * nh + 2 * p
                l1 = l0 + 1
                lanes = slice(p * LANES, (p + 1) * LANES)
                gcol = jnp.where(first_half, gam[:, l0:l0 + 1], gam[:, l1:l1 + 1])
                grow = jnp.where(first_half, gam_t[l0:l0 + 1, :], gam_t[l1:l1 + 1, :])
                bcol = jnp.where(first_half, bc[:, 2 * nh + l0:2 * nh + l0 + 1],
                                 bc[:, 2 * nh + l1:2 * nh + l1 + 1])
                dec = jnp.where(incl, jnp.exp(jnp.where(incl, gcol - grow, 0.0)), 0.0)
                qg = qkv_s[rows, p * LANES:(p + 1) * LANES]
                kg = qkv_s[rows, BR_WIDTH + p * LANES:BR_WIDTH + (p + 1) * LANES]
                vg = qkv_s[rows, 2 * BR_WIDTH + p * LANES:2 * BR_WIDTH + (p + 1) * LANES]
                kb = kg * bcol
                kstack = _pair_stack(kg, lane2).astype(BF16)
                kk = _dot_nt(kb.astype(BF16), kstack)
                qk = _dot_nt(qg.astype(BF16), kstack) * dec
                pm = -jnp.where(strict, kk * dec, 0.0)
                t = eye2 + pm
                bd = _pair_blockdiag(pm, bdmask).astype(BF16)
                for _ in range(5):
                    pm = _dot(pm.astype(BF16), bd)
                    bd = _pair_blockdiag(pm, bdmask).astype(BF16)
                    t = t + _dot(t.astype(BF16), bd)
                eg = jnp.exp(gcol)
                tb = t.astype(BF16)
                u = _dot(tb, _pair_stack(vg * bcol, lane2).astype(BF16))
                w = _dot(tb, _pair_stack(kb * eg, lane2).astype(BF16))
                glast = gcol[c - 1:c, :] if d == 0 else gcol[0:1, :]
                u_ref[0, d, rows, lanes] = u
                w_ref[0, d, rows, lanes] = w
                qk_ref[0, d, rows, lanes] = qk
                qd_ref[0, d, rows, lanes] = qg * eg
                kd_ref[0, d, rows, lanes] = kg * jnp.exp(glast - gcol)
                el_ref[0, d, pl.ds(pl.multiple_of(ch * SUBLANES, SUBLANES), SUBLANES), lanes] = (
                    jnp.broadcast_to(jnp.exp(glast), (SUBLANES, LANES)))
        return carry

    lax.fori_loop(0, tm // c, chunk, 0)


def _dn_chunk(pd, pab, conv_w, alog, dtb, n_ctx):
    b, s, _ = pd.shape
    tm = TOKEN_TILE
    nt = s // tm
    nsub = tm // SUBLANES
    last_blk = s // SUBLANES - 1
    full = lambda a: pl.BlockSpec(a.shape, lambda i, j: (0,) * a.ndim)
    dir_tok = lambda: pl.BlockSpec((1, 2, tm, BR_WIDTH), lambda i, j: (i, 0, j, 0))
    dir_shape = jax.ShapeDtypeStruct((b, 2, s, BR_WIDTH), F32)
    el_rows = (tm // DN_CHUNK) * SUBLANES
    return pl.pallas_call(
        functools.partial(_dn_chunk_body, n_ctx_tiles=n_ctx // tm, n_tiles=nt),
        grid=(b, nt),
        in_specs=[pl.BlockSpec((1, tm, PD_COLS), lambda i, j: (i, j, 0)),
                  pl.BlockSpec((1, SUBLANES, PD_COLS), lambda i, j: (i, jnp.maximum(j * nsub - 1, 0), 0)),
                  pl.BlockSpec((1, SUBLANES, PD_COLS),
                               lambda i, j: (i, jnp.minimum((j + 1) * nsub, last_blk), 0)),
                  pl.BlockSpec((1, tm, PAB_COLS), lambda i, j: (i, j, 0)),
                  full(conv_w), full(alog), full(dtb)],
        out_specs=[dir_tok(), dir_tok(), dir_tok(), dir_tok(), dir_tok(),
                   pl.BlockSpec((1, 2, el_rows, BR_WIDTH), lambda i, j: (i, 0, j, 0))],
        out_shape=[dir_shape] * 5 + [jax.ShapeDtypeStruct((b, 2, nt * el_rows, BR_WIDTH), F32)],
        scratch_shapes=[pltpu.VMEM((tm, PD_COLS), F32), pltpu.VMEM((tm, PAB_COLS), F32),
                        pltpu.VMEM((tm, PAB_COLS), F32)],
        compiler_params=_cparams(("arbitrary", "arbitrary")),
        name="dn_chunk",
    )(pd, pd, pd, pab, conv_w, alog, dtb)


def _dn_scan_body(uf, wf, qkf, qdf, kdf, elf, ub, wb, qkb, qdb, kdb, elb, of_ref, ob_ref, s_ref):
    tm = uf.shape[2]
    c = DN_CHUNK
    n_chunks = tm // c

    @pl.when(pl.program_id(1) == 0)
    def _():
        s_ref[...] = jnp.zeros_like(s_ref)

    bdmask = _iota((LANES, LANES), 0) // c == _iota((LANES, LANES), 1) // c
    views = ((uf, wf, qkf, qdf, kdf, elf, of_ref), (ub, wb, qkb, qdb, kdb, elb, ob_ref))
    for step in range(n_chunks):
        for d in range(2):
            u_r, w_r, qk_r, qd_r, kd_r, el_r, o_r = views[d]
            ch = step if d == 0 else n_chunks - 1 - step
            rows = slice(ch * c, (ch + 1) * c)
            for p in range(DN_HEADS // 2):
                lanes = slice(p * LANES, (p + 1) * LANES)
                sbd = s_ref[d, p]
                sb = sbd.astype(BF16)
                vnew = u_r[0, 0, rows, lanes] - _dot(w_r[0, 0, rows, lanes].astype(BF16), sb)
                vb = vnew.astype(BF16)
                vbd = jnp.where(bdmask, jnp.concatenate([vb, vb], axis=0), jnp.zeros((), BF16))
                o_r[0, rows, lanes] = (_dot(qd_r[0, 0, rows, lanes].astype(BF16), sb)
                                       + _dot(qk_r[0, 0, rows, lanes].astype(BF16), vbd))
                upd = _dot_tn(kd_r[0, 0, rows, lanes].astype(BF16), vb)
                el = el_r[0, 0, ch * SUBLANES:ch * SUBLANES + 1, lanes]
                s_ref[d, p] = sbd * el + jnp.where(bdmask, upd, 0.0)


def _dn_scan(u, w, qk, qd, kd, el, n_ctx):
    b, _, s, _ = u.shape
    tm = TOKEN_TILE
    nt = s // tm
    nctx = n_ctx // tm
    el_rows = (tm // DN_CHUNK) * SUBLANES

    def rev(j):
        return jnp.where(j < nctx, nctx - 1 - j, nt - 1 - (j - nctx))

    fwd = lambda rows: pl.BlockSpec((1, 1, rows, BR_WIDTH), lambda i, j: (i, 0, j, 0))
    bwd = lambda rows: pl.BlockSpec((1, 1, rows, BR_WIDTH), lambda i, j: (i, 1, rev(j), 0))
    in_specs = [fwd(tm)] * 5 + [fwd(el_rows)] + [bwd(tm)] * 5 + [bwd(el_rows)]
    out_shape = jax.ShapeDtypeStruct((b, s, BR_WIDTH), F32)
    return pl.pallas_call(
        _dn_scan_body,
        grid=(b, nt),
        in_specs=in_specs,
        out_specs=[pl.BlockSpec((1, tm, BR_WIDTH), lambda i, j: (i, j, 0)),
                   pl.BlockSpec((1, tm, BR_WIDTH), lambda i, j: (i, rev(j), 0))],
        out_shape=[out_shape, out_shape],
        scratch_shapes=[pltpu.VMEM((2, DN_HEADS // 2, LANES, LANES), F32)],
        compiler_params=_cparams(("arbitrary", "arbitrary")),
        name="dn_scan",
    )(u, w, qk, qd, kd, el, u, w, qk, qd, kd, el)


def _merge_body(x_ref, mod_ref, h_ref, pg_ref, ya_ref, yb_ref, yc_ref, of_ref, ob_ref, dng_ref,
                wg_ref, wb_ref, wo_ref, lng_ref, lnb_ref, o_ref, *, alpha):
    hb = h_ref[0]
    o = of_ref[0] + ob_ref[0]
    ms = _dot_f32(o * o, _block_ones(BR_WIDTH, DN_DV)) * (1.0 / DN_DV)
    yd = o * lax.rsqrt(ms + LN_EPS) * dng_ref[...]
    pg = pg_ref[0]
    sg = pg * _sigmoid(pg)
    acc = None
    for i, y in enumerate((ya_ref[0], yb_ref[0], yc_ref[0], yd)):
        t = (y * sg[:, i * BR_WIDTH:(i + 1) * BR_WIDTH]).astype(BF16)
        term = _sigmoid(_dot(hb, wg_ref[i])) * _dot(t, wb_ref[i])
        acc = term if acc is None else acc + term
    out = _dot(acc.astype(BF16), wo_ref[...])
    gt = mod_ref[0, 0, 2:3, :]
    r = alpha * x_ref[0] + gt * out
    mu = jnp.mean(r, -1, keepdims=True)
    rc = r - mu
    var = jnp.mean(rc * rc, -1, keepdims=True)
    o_ref[0] = rc * lax.rsqrt(var + LN_EPS) * lng_ref[...] + lnb_ref[...]


def _merge(xs, mod, h, pg, ya, yb, yc, o_f, o_b, dng, wg, wb, wo, lng, lnb, n_ctx_tiles, skip_tiles, alpha):
    b, s, d = xs.shape
    tm = TOKEN_TILE
    nt = s // tm - skip_tiles
    tok = lambda width: pl.BlockSpec((1, tm, width), lambda i, j: (i, j + skip_tiles, 0))
    full = lambda a: pl.BlockSpec(a.shape, lambda i, j: (0,) * a.ndim)
    return pl.pallas_call(
        functools.partial(_merge_body, alpha=alpha),
        grid=(b, nt),
        in_specs=[tok(d),
                  pl.BlockSpec((1, 1, 3, d),
                               lambda i, j: (i, jnp.where(j + skip_tiles >= n_ctx_tiles, 1, 0), 0, 0)),
                  tok(d), tok(PG_COLS), tok(BR_WIDTH), tok(BR_WIDTH), tok(BR_WIDTH), tok(BR_WIDTH),
                  tok(BR_WIDTH), full(dng), full(wg), full(wb), full(wo), full(lng), full(lnb)],
        out_specs=pl.BlockSpec((1, tm, d), lambda i, j: (i, j, 0)),
        out_shape=jax.ShapeDtypeStruct((b, nt * tm, d), F32),
        compiler_params=_cparams(("arbitrary", "arbitrary")),
        name="merge",
    )(xs, mod, h, pg, ya, yb, yc, o_f, o_b, dng, wg, wb, wo, lng, lnb)


def _rope_swap_index(n_blocks):
    blk = jnp.array(list(range(8, 16)) + list(range(0, 8)) + list(range(24, 32)) + list(range(16, 24)))
    return (jnp.arange(n_blocks)[:, None] * 32 + blk[None, :]).reshape(-1)


def _pack_w_in(w):
    d = w.shape[0]
    o = 0
    cq = w[:, o:o + MLA_Q_LORA]; o += MLA_Q_LORA
    ckv = w[:, o:o + MLA_KV_LORA]; o += MLA_KV_LORA
    kr = w[:, o:o + MLA_ROPE]; o += MLA_ROPE
    pb = w[:, o:o + GMLP_COLS]; o += GMLP_COLS
    dq = w[:, o:o + 256]; dk = w[:, o + 256:o + 512]; dv = w[:, o + 512:o + 768]; o += DIFF_COLS
    dn_qkv = w[:, o:o + 3 * BR_WIDTH]; dn_ab = w[:, o + 3 * BR_WIDTH:o + DN_COLS]; o += DN_COLS
    pg = w[:, o:]
    z = lambda n: jnp.zeros((d, n), w.dtype)
    sw1 = _rope_swap_index(1)
    sw8 = _rope_swap_index(8)
    place = lambda a: jnp.concatenate([z(MLA_NOPE), a, z(LANES - MLA_NOPE - MLA_ROPE)], axis=1)
    packed = jnp.concatenate(
        [cq, ckv, place(kr), place(kr[:, sw1]), pb, dq, dk, dv, dq[:, sw8], dk[:, sw8], dn_qkv,
         dn_ab, z(PAB_COLS - 4 * DN_HEADS), pg], axis=1)
    return packed.astype(BF16)


def _pack_mla_weights(w_uq, w_ukv):
    dq = MLA_NOPE + MLA_ROPE
    wq = w_uq.reshape(MLA_Q_LORA, MLA_HEADS, dq)
    zq = jnp.zeros((MLA_Q_LORA, MLA_HEADS, LANES - dq), w_uq.dtype)
    wq_p = jnp.concatenate([wq, zq], axis=2).reshape(MLA_Q_LORA, MLA_HEADS * LANES)
    rope_sw = wq[:, :, MLA_NOPE:][:, :, _rope_swap_index(1)]
    wqs_p = jnp.concatenate([jnp.zeros((MLA_Q_LORA, MLA_HEADS, MLA_NOPE), w_uq.dtype), rope_sw, zq],
                            axis=2).reshape(MLA_Q_LORA, MLA_HEADS * LANES)
    wkv = w_ukv.reshape(MLA_KV_LORA, MLA_HEADS, MLA_NOPE + MLA_V)
    zk = jnp.zeros((MLA_KV_LORA, MLA_HEADS, LANES - MLA_NOPE), w_ukv.dtype)
    wk_p = jnp.concatenate([wkv[:, :, :MLA_NOPE], zk], axis=2).reshape(MLA_KV_LORA, MLA_HEADS * LANES)
    wv_p = wkv[:, :, MLA_NOPE:].reshape(MLA_KV_LORA, MLA_HEADS * MLA_V)
    return wq_p.astype(BF16), wqs_p.astype(BF16), wk_p.astype(BF16), wv_p.astype(BF16)


def _rope_tables(n, n_ctx):
    rows = n // GRID_W
    row = jnp.repeat(jnp.arange(rows, dtype=F32), GRID_W)
    col = jnp.tile(jnp.arange(GRID_W, dtype=F32), rows)
    axis_dim = MLA_ROPE // 2
    inv_freq = ROPE_BASE ** (-jnp.arange(0, axis_dim, 2, dtype=F32) / axis_dim)
    ar = row[:, None] * inv_freq
    ac = col[:, None] * inv_freq
    cos32 = jnp.concatenate([jnp.cos(ar), jnp.cos(ar), jnp.cos(ac), jnp.cos(ac)], axis=1)
    sin32 = jnp.concatenate([-jnp.sin(ar), jnp.sin(ar), -jnp.sin(ac), jnp.sin(ac)], axis=1)
    cos32 = jnp.concatenate([jnp.ones((n_ctx, 32), F32), cos32], axis=0)
    sin32 = jnp.concatenate([jnp.zeros((n_ctx, 32), F32), sin32], axis=0)
    s = n + n_ctx
    one = lambda w: jnp.ones((s, w), F32)
    zero = lambda w: jnp.zeros((s, w), F32)
    mla_cos = jnp.concatenate([one(MLA_NOPE), cos32, one(LANES - MLA_NOPE - MLA_ROPE)], axis=1)
    mla_sin = jnp.concatenate([zero(MLA_NOPE), sin32, zero(LANES - MLA_NOPE - MLA_ROPE)], axis=1)
    diff_cos = jnp.tile(cos32, (1, LANES // 32))
    diff_sin = jnp.tile(sin32, (1, LANES // 32))
    return mla_cos, mla_sin, diff_cos, diff_sin


def _pad_lanes(a, width=LANES):
    return jnp.concatenate([a, jnp.zeros(a.shape[:-1] + (width - a.shape[-1],), a.dtype)], axis=-1)


def kernel(x, c, ctx, c_ctx, w_mod, b_mod, w_in, mla_q_norm, mla_w_uq, mla_kv_norm, mla_w_ukv, gmlp_ln_g, gmlp_w_s, gmlp_b_s, diff_lq1, diff_lk1, diff_lq2, diff_lk2, diff_norm_g, dn_conv_w, dn_a_log, dn_dt_bias, dn_norm_g, w_gate, w_branch, w_out, ln_g, ln_b):
    b, n, d = x.shape
    n_ctx = ctx.shape[1]
    depth = w_mod.shape[0]
    tm = TOKEN_TILE
    assert d == D_MODEL and n % tm == 0 and n_ctx % tm == 0 and n % GRID_W == 0
    assert tm == ATTN_Q_TILE == ATTN_K_TILE
    alpha = (2 * depth) ** 0.25
    n_ctx_tiles = n_ctx // tm

    xs = jnp.concatenate([ctx, x], axis=1)
    mla_cos, mla_sin, diff_cos, diff_sin = _rope_tables(n, n_ctx)

    rows = ((b + 1 + SUBLANES - 1) // SUBLANES) * SUBLANES
    cc = jnp.concatenate([c, c_ctx[None, :], jnp.zeros((rows - b - 1, d), F32)], axis=0)
    mod_all = _modulation(cc, w_mod, b_mod)

    for l in range(depth):
        last = l == depth - 1
        lam_init = 0.8 - 0.6 * math.exp(-0.3 * l)
        mod_l = mod_all[l].reshape(rows, 3, d)
        mod = jnp.stack([jnp.broadcast_to(mod_l[b][None], (b, 3, d)), mod_l[:b]], axis=1)

        h, pa, pb, pc, pd, pab, pg = _inproj(xs, mod, _pack_w_in(w_in[l]), n_ctx_tiles)

        wq, wqs, wk, wv = _pack_mla_weights(mla_w_uq[l], mla_w_ukv[l])
        q_a, k_a, vt_a = _mla_prep(pa, mla_cos, mla_sin, mla_q_norm[l][None, :], mla_kv_norm[l][None, :],
                                   wq, wqs, wk, wv)
        ya = _mla_attn(q_a, k_a, vt_a, n_ctx)

        bias = jnp.repeat(gmlp_b_s[l].T, BR_WIDTH // GMLP_GROUPS, axis=1)
        yb = _gmlp(pb, gmlp_ln_g[l][None, :], gmlp_w_s[l].astype(BF16), bias)

        q_c, k_c, vt_c = _diff_prep(pc, diff_cos, diff_sin)
        lqk = _pad_lanes(jnp.stack([diff_lq1[l], diff_lk1[l], diff_lq2[l], diff_lk2[l]], axis=0))
        yc = _diff_attn(q_c, k_c, vt_c, lqk, diff_norm_g[l][:, None], n_ctx, lam_init)

        conv_w = jnp.concatenate([dn_conv_w[l], jnp.zeros((SUBLANES - 3, PD_COLS), F32)], axis=0)
        alog = _pad_lanes(dn_a_log[l].reshape(1, 2 * DN_HEADS))
        dtb = _pad_lanes(dn_dt_bias[l].reshape(1, 2 * DN_HEADS))
        u, w, qk, qd, kd, el = _dn_chunk(pd, pab, conv_w, alog, dtb, n_ctx)
        o_f, o_b = _dn_scan(u, w, qk, qd, kd, el, n_ctx)

        dng = jnp.tile(dn_norm_g[l], DN_HEADS)[None, :]
        xs = _merge(xs, mod, h, pg, ya, yb, yc, o_f, o_b, dng, w_gate[l].astype(BF16),
                    w_branch[l].astype(BF16), w_out[l].astype(BF16), ln_g[l][None, :], ln_b[l][None, :],
                    n_ctx_tiles, n_ctx_tiles if last else 0, alpha)
    return xs
```

```python
import functools
import math

import jax
import jax.numpy as jnp
from jax import lax
from jax.experimental import pallas as pl
from jax.experimental.pallas import tpu as pltpu

F32 = jnp.float32
BF16 = jnp.bfloat16

D_MODEL = 1024
GRID_W = 64
N_BRANCH = 4
BR_WIDTH = 256
MLA_HEADS = 4
MLA_NOPE = 64
MLA_ROPE = 32
MLA_V = 64
MLA_Q_LORA = 256
MLA_KV_LORA = 128
GMLP_GROUPS = 4
GMLP_CHUNK = 128
DIFF_HEADS = 4
DIFF_D = 32
DN_HEADS = 4
DN_DK = 64
DN_DV = 64
DN_CHUNK = 64
ROPE_BASE = 10000.0
LN_EPS = 1e-6

MLA_COLS = MLA_Q_LORA + MLA_KV_LORA + MLA_ROPE
GMLP_COLS = 2 * BR_WIDTH
DIFF_COLS = 3 * DIFF_HEADS * 2 * DIFF_D
DN_COLS = 3 * BR_WIDTH + 4 * DN_HEADS

LANES = 128
SUBLANES = 8
VMEM_LIMIT_BYTES = 56 * 1024 * 1024

TOKEN_TILE = 256
ATTN_Q_TILE = 256
ATTN_K_TILE = 256
DN_CHUNKS_PER_GROUP = 4
ATTN_STEPS_PER_TRIP = 32
ATTN_LOOKAHEAD = 4
NEG_BIG = -1e30
VT_ROWS = 80
LOG2_E = 1.4426950408889634

PA_COLS = 640
PB_COLS = 512
PC_COLS = 1280
PD_COLS = 768
PAB_COLS = 128
PG_COLS = 1024
PACK_SPLITS = (PA_COLS, PB_COLS, PC_COLS, PD_COLS, PAB_COLS, PG_COLS)
PACK_COLS = sum(PACK_SPLITS)


def _cparams(semantics):
    return pltpu.CompilerParams(dimension_semantics=semantics, vmem_limit_bytes=VMEM_LIMIT_BYTES)


def _dot(a, b):
    return jnp.dot(a, b, preferred_element_type=F32)


def _dot_nt(a, b):
    return lax.dot_general(a, b, (((1,), (1,)), ((), ())), preferred_element_type=F32)


def _dot_tn(a, b):
    return lax.dot_general(a, b, (((0,), (0,)), ((), ())), preferred_element_type=F32)


def _dot_f32(a, b):
    return jnp.dot(a, b, preferred_element_type=F32, precision=lax.Precision.HIGHEST)


def _sigmoid(x):
    return 1.0 / (1.0 + jnp.exp(-x))


def _iota(shape, axis):
    return lax.broadcasted_iota(jnp.int32, shape, axis)


def _block_ones(n, blk):
    return (_iota((n, n), 0) // blk == _iota((n, n), 1) // blk).astype(F32)


def _mod_body(c_ref, w_ref, b_ref, o_ref):
    c = c_ref[...]
    s = (c * _sigmoid(c)).astype(BF16)
    o_ref[0] = _dot(s, w_ref[0].astype(BF16)) + b_ref[0]


def _modulation(cc, w_mod, b_mod):
    depth, d, n3 = w_mod.shape
    rows = cc.shape[0]
    tn = 1024
    return pl.pallas_call(
        _mod_body,
        grid=(depth, n3 // tn),
        in_specs=[pl.BlockSpec((rows, d), lambda l, j: (0, 0)),
                  pl.BlockSpec((1, d, tn), lambda l, j: (l, 0, j)),
                  pl.BlockSpec((1, 1, tn), lambda l, j: (l, 0, j))],
        out_specs=pl.BlockSpec((1, rows, tn), lambda l, j: (l, 0, j)),
        out_shape=jax.ShapeDtypeStruct((depth, rows, n3), F32),
        compiler_params=_cparams(("arbitrary", "arbitrary")),
        name="modulation",
    )(cc, w_mod, b_mod.reshape(depth, 1, n3))


def _inproj_body(x_ref, mod_ref, w_ref, h_ref, *out_refs):
    x = x_ref[0]
    mu = jnp.mean(x, -1, keepdims=True)
    xc = x - mu
    var = jnp.mean(xc * xc, -1, keepdims=True)
    xn = xc * lax.rsqrt(var + LN_EPS)
    sh = mod_ref[0, 0, 0:1, :]
    sc = mod_ref[0, 0, 1:2, :]
    hb = (xn * (1.0 + sc) + sh).astype(BF16)
    h_ref[0] = hb
    off = 0
    for ref, width in zip(out_refs, PACK_SPLITS):
        ref[0] = _dot(hb, w_ref[:, off:off + width])
        off += width


def _inproj(xs, mod, w_pack, n_ctx_tiles):
    b, s, d = xs.shape
    tm = TOKEN_TILE
    tok = lambda width, dt: jax.ShapeDtypeStruct((b, s, width), dt)
    tok_spec = lambda width: pl.BlockSpec((1, tm, width), lambda i, j: (i, j, 0))
    return pl.pallas_call(
        _inproj_body,
        grid=(b, s // tm),
        in_specs=[tok_spec(d),
                  pl.BlockSpec((1, 1, 3, d), lambda i, j: (i, jnp.where(j >= n_ctx_tiles, 1, 0), 0, 0)),
                  pl.BlockSpec((d, PACK_COLS), lambda i, j: (0, 0))],
        out_specs=[tok_spec(d)] + [tok_spec(w) for w in PACK_SPLITS],
        out_shape=[tok(d, BF16)] + [tok(w, F32) for w in PACK_SPLITS],
        compiler_params=_cparams(("arbitrary", "arbitrary")),
        name="inproj",
    )(xs, mod, w_pack)


def _rms(x, g):
    return x * lax.rsqrt(jnp.mean(x * x, -1, keepdims=True) + LN_EPS) * g


def _mla_prep_body(pa_ref, cos_ref, sin_ref, qn_ref, kvn_ref, wq_ref, wqs_ref, wk_ref, wv_ref,
                   q_ref, k_ref, vt_ref, *, scale):
    pa = pa_ref[0]
    cq = pa[:, 0:MLA_Q_LORA]
    ckv = pa[:, MLA_Q_LORA:MLA_Q_LORA + MLA_KV_LORA]
    kr = pa[:, 384:512]
    kr_sw = pa[:, 512:640]
    cos = cos_ref[...]
    sin = sin_ref[...]
    cos4 = jnp.concatenate([cos] * MLA_HEADS, axis=1)
    sin4 = jnp.concatenate([sin] * MLA_HEADS, axis=1)
    cqn = _rms(cq, qn_ref[...]).astype(BF16)
    q = _dot(cqn, wq_ref[...])
    q_sw = _dot(cqn, wqs_ref[...])
    q_ref[0] = ((q * cos4 + q_sw * sin4) * scale).astype(BF16)
    ckvn = _rms(ckv, kvn_ref[...]).astype(BF16)
    kn = _dot(ckvn, wk_ref[...])
    kr_rot = kr * cos + kr_sw * sin
    k_ref[0] = (kn + jnp.concatenate([kr_rot] * MLA_HEADS, axis=1)).astype(BF16)
    v = _dot(ckvn, wv_ref[...])
    vt_ref[0, 0] = _vt_with_ones(v, MLA_HEADS, MLA_V)


def _mla_prep(pa, cos_t, sin_t, qn, kvn, wq, wqs, wk, wv):
    b, s, _ = pa.shape
    tm = ATTN_K_TILE
    hp = MLA_HEADS * LANES
    full = lambda a: pl.BlockSpec(a.shape, lambda i, j: (0,) * a.ndim)
    scale = (MLA_NOPE + MLA_ROPE) ** -0.5 * LOG2_E
    return pl.pallas_call(
        functools.partial(_mla_prep_body, scale=scale),
        grid=(b, s // tm),
        in_specs=[pl.BlockSpec((1, tm, PA_COLS), lambda i, j: (i, j, 0)),
                  pl.BlockSpec((tm, LANES), lambda i, j: (j, 0)),
                  pl.BlockSpec((tm, LANES), lambda i, j: (j, 0)),
                  full(qn), full(kvn), full(wq), full(wqs), full(wk), full(wv)],
        out_specs=[pl.BlockSpec((1, tm, hp), lambda i, j: (i, j, 0)),
                   pl.BlockSpec((1, tm, hp), lambda i, j: (i, j, 0)),
                   pl.BlockSpec((1, 1, 4 * VT_ROWS, tm), lambda i, j: (i, j, 0, 0))],
        out_shape=[jax.ShapeDtypeStruct((b, s, hp), BF16),
                   jax.ShapeDtypeStruct((b, s, hp), BF16),
                   jax.ShapeDtypeStruct((b, s // tm, 4 * VT_ROWS, tm), BF16)],
        compiler_params=_cparams(("arbitrary", "arbitrary")),
        name="mla_prep",
    )(pa, cos_t, sin_t, qn, kvn, wq, wqs, wk, wv)


def _vt_with_ones(v, heads, dv):
    tm = v.shape[0]
    vt = v.T
    aug = (_iota((VT_ROWS - dv, tm), 0) == 0).astype(F32)
    pieces = []
    for h in range(heads):
        pieces += [vt[h * dv:(h + 1) * dv], aug]
    return jnp.concatenate(pieces, axis=0).astype(BF16)


def _tiles_per_iter(n_tiles, n_chain):
    t = max(1, ATTN_STEPS_PER_TRIP // n_chain)
    while n_tiles % t:
        t -= 1
    return t


def _flash_chains(qt_s, k_ref, vt_ref, k_lanes, v_rows, n_k, n_first, tiles_per_iter, m_s, acc_s):
    tk = ATTN_K_TILE
    m_s[...] = jnp.full(m_s.shape, NEG_BIG, F32)
    acc_s[...] = jnp.zeros(acc_s.shape, F32)

    n_chain = len(k_lanes)

    def scores(c, kt):
        k_rows = pl.ds(pl.multiple_of(kt * tk, tk), tk)
        return _dot(k_ref[0, k_rows, k_lanes[c][0]:k_lanes[c][1]], qt_s[c])

    def run(kt0, n_iter, tiles_per_iter):
        per_iter = tiles_per_iter * n_chain

        def step(it, ahead):
            ahead = list(ahead)
            base = kt0 + it * tiles_per_iter
            for idx in range(per_iter):
                kt, c = base + idx // n_chain, idx % n_chain
                s = ahead.pop(0)
                nxt = idx + ATTN_LOOKAHEAD
                ahead.append(scores(nxt % n_chain, jnp.minimum(base + nxt // n_chain, n_k - 1)))
                m_old = m_s[c]
                m_new = jnp.maximum(m_old, jnp.max(s, axis=0, keepdims=True))
                p = jnp.exp2(s - m_new).astype(BF16)
                vt = vt_ref[0, kt, v_rows[c][0]:v_rows[c][1], :]
                acc_s[c] = jnp.exp2(m_old - m_new) * acc_s[c] + _dot(vt, p)
                m_s[c] = m_new
            return tuple(ahead)

        init = tuple(scores(i % n_chain, jnp.minimum(kt0 + i // n_chain, n_k - 1))
                     for i in range(ATTN_LOOKAHEAD))
        lax.fori_loop(0, n_iter, step, init)

    run(0, n_first, 1)
    run(n_first, (n_k - n_first) // tiles_per_iter, tiles_per_iter)


def _mla_attn_body(q_ref, k_ref, vt_ref, o_ref, qt_s, m_s, acc_s, *, n_ctx_q_tiles, n_ctx_k_tiles,
                   n_k_tiles):
    j = pl.program_id(1)
    n_k = jnp.where(j < n_ctx_q_tiles, n_ctx_k_tiles, n_k_tiles)
    k_lanes = [(h * LANES, (h + 1) * LANES) for h in range(MLA_HEADS)]
    v_rows = [(h * VT_ROWS, (h + 1) * VT_ROWS) for h in range(MLA_HEADS)]
    for h in range(MLA_HEADS):
        qt_s[h] = q_ref[0, :, h * LANES:(h + 1) * LANES].astype(F32).T.astype(BF16)
    _flash_chains(qt_s, k_ref, vt_ref, k_lanes, v_rows, n_k, n_ctx_k_tiles,
                  _tiles_per_iter(n_k_tiles - n_ctx_k_tiles, MLA_HEADS), m_s, acc_s)
    outs = []
    for h in range(MLA_HEADS):
        acc = acc_s[h]
        outs.append(acc[0:MLA_V] / acc[MLA_V:MLA_V + 1])
    o_ref[0] = jnp.concatenate(outs, axis=0).T


def _mla_attn(q, k, vt, n_ctx):
    b, s, hp = q.shape
    tq, tk = ATTN_Q_TILE, ATTN_K_TILE
    return pl.pallas_call(
        functools.partial(_mla_attn_body, n_ctx_q_tiles=n_ctx // tq, n_ctx_k_tiles=n_ctx // tk,
                          n_k_tiles=s // tk),
        grid=(b, s // tq),
        in_specs=[pl.BlockSpec((1, tq, hp), lambda i, j: (i, j, 0)),
                  pl.BlockSpec((1, s, hp), lambda i, j: (i, 0, 0)),
                  pl.BlockSpec((1, s // tk, MLA_HEADS * VT_ROWS, tk), lambda i, j: (i, 0, 0, 0))],
        out_specs=pl.BlockSpec((1, tq, BR_WIDTH), lambda i, j: (i, j, 0)),
        out_shape=jax.ShapeDtypeStruct((b, s, BR_WIDTH), F32),
        scratch_shapes=[pltpu.VMEM((MLA_HEADS, LANES, tq), BF16), pltpu.VMEM((MLA_HEADS, 1, tq), F32),
                        pltpu.VMEM((MLA_HEADS, VT_ROWS, tq), F32)],
        compiler_params=_cparams(("arbitrary", "arbitrary")),
        name="mla_attn",
    )(q, k, vt)


def _diff_prep_body(pc_ref, cos_ref, sin_ref, q_ref, k_ref, vt_ref, *, scale):
    pc = pc_ref[0]
    cos = jnp.concatenate([cos_ref[...]] * 2, axis=1)
    sin = jnp.concatenate([sin_ref[...]] * 2, axis=1)
    q = pc[:, 0:256]
    k = pc[:, 256:512]
    v = pc[:, 512:768]
    q_sw = pc[:, 768:1024]
    k_sw = pc[:, 1024:1280]
    q_ref[0] = ((q * cos + q_sw * sin) * scale).astype(BF16)
    k_ref[0] = (k * cos + k_sw * sin).astype(BF16)
    vt_ref[0, 0] = _vt_with_ones(v, DIFF_HEADS, 2 * DIFF_D)


def _diff_prep(pc, cos_t, sin_t):
    b, s, _ = pc.shape
    tm = ATTN_K_TILE
    return pl.pallas_call(
        functools.partial(_diff_prep_body, scale=DIFF_D ** -0.5 * LOG2_E),
        grid=(b, s // tm),
        in_specs=[pl.BlockSpec((1, tm, PC_COLS), lambda i, j: (i, j, 0)),
                  pl.BlockSpec((tm, LANES), lambda i, j: (j, 0)),
                  pl.BlockSpec((tm, LANES), lambda i, j: (j, 0))],
        out_specs=[pl.BlockSpec((1, tm, BR_WIDTH), lambda i, j: (i, j, 0)),
                   pl.BlockSpec((1, tm, BR_WIDTH), lambda i, j: (i, j, 0)),
                   pl.BlockSpec((1, 1, 4 * VT_ROWS, tm), lambda i, j: (i, j, 0, 0))],
        out_shape=[jax.ShapeDtypeStruct((b, s, BR_WIDTH), BF16),
                   jax.ShapeDtypeStruct((b, s, BR_WIDTH), BF16),
                   jax.ShapeDtypeStruct((b, s // tm, 4 * VT_ROWS, tm), BF16)],
        compiler_params=_cparams(("arbitrary", "arbitrary")),
        name="diff_prep",
    )(pc, cos_t, sin_t)


def _diff_attn_body(q_ref, k_ref, vt_ref, lqk_ref, g_ref, o_ref, qm_s, m_s, acc_s, *, n_ctx_q_tiles,
                    n_ctx_k_tiles, n_k_tiles, lam_init):
    tq = q_ref.shape[1]
    j = pl.program_id(1)
    n_k = jnp.where(j < n_ctx_q_tiles, n_ctx_k_tiles, n_k_tiles)
    lqk = lqk_ref[...]
    lam = (jnp.exp(jnp.sum(lqk[0:1] * lqk[1:2], axis=1, keepdims=True))
           - jnp.exp(jnp.sum(lqk[2:3] * lqk[3:4], axis=1, keepdims=True)) + lam_init)
    row = _iota((LANES, tq), 0)
    dv = 2 * DIFF_D
    k_lanes, v_rows = [], []
    for grp in range(2 * DIFF_HEADS * DIFF_D // LANES):
        qgt = q_ref[0, :, grp * LANES:(grp + 1) * LANES].astype(F32).T
        for sub in range(LANES // DIFF_D):
            blk = grp * (LANES // DIFF_D) + sub
            keep = (row >= sub * DIFF_D) & (row < (sub + 1) * DIFF_D)
            qm_s[blk] = jnp.where(keep, qgt, 0.0).astype(BF16)
            k_lanes.append((grp * LANES, (grp + 1) * LANES))
            v_rows.append(((blk // 2) * VT_ROWS, (blk // 2 + 1) * VT_ROWS))
    _flash_chains(qm_s, k_ref, vt_ref, k_lanes, v_rows, n_k, n_ctx_k_tiles,
                  _tiles_per_iter(n_k_tiles - n_ctx_k_tiles, 2 * DIFF_HEADS), m_s, acc_s)
    outs = []
    for h in range(DIFF_HEADS):
        a1 = acc_s[2 * h]
        a2 = acc_s[2 * h + 1]
        o = a1[0:dv] / a1[dv:dv + 1] - lam * (a2[0:dv] / a2[dv:dv + 1])
        o = o * lax.rsqrt(jnp.mean(o * o, axis=0, keepdims=True) + LN_EPS)
        outs.append(o * g_ref[...] * (1.0 - lam_init))
    o_ref[0] = jnp.concatenate(outs, axis=0).T


def _diff_attn(q, k, vt, lqk, g_col, n_ctx, lam_init):
    b, s, _ = q.shape
    tq, tk = ATTN_Q_TILE, ATTN_K_TILE
    return pl.pallas_call(
        functools.partial(_diff_attn_body, n_ctx_q_tiles=n_ctx // tq, n_ctx_k_tiles=n_ctx // tk,
                          n_k_tiles=s // tk, lam_init=lam_init),
        grid=(b, s // tq),
        in_specs=[pl.BlockSpec((1, tq, BR_WIDTH), lambda i, j: (i, j, 0)),
                  pl.BlockSpec((1, s, BR_WIDTH), lambda i, j: (i, 0, 0)),
                  pl.BlockSpec((1, s // tk, DIFF_HEADS * VT_ROWS, tk), lambda i, j: (i, 0, 0, 0)),
                  pl.BlockSpec(lqk.shape, lambda i, j: (0, 0)),
                  pl.BlockSpec(g_col.shape, lambda i, j: (0, 0))],
        out_specs=pl.BlockSpec((1, tq, BR_WIDTH), lambda i, j: (i, j, 0)),
        out_shape=jax.ShapeDtypeStruct((b, s, BR_WIDTH), F32),
        scratch_shapes=[pltpu.VMEM((2 * DIFF_HEADS, LANES, tq), BF16),
                        pltpu.VMEM((2 * DIFF_HEADS, 1, tq), F32),
                        pltpu.VMEM((2 * DIFF_HEADS, VT_ROWS, tq), F32)],
        compiler_params=_cparams(("arbitrary", "arbitrary")),
        name="diff_attn",
    )(q, k, vt, lqk, g_col)


def _gelu_tanh(x):
    return 0.5 * x * (1.0 + jnp.tanh(math.sqrt(2.0 / math.pi) * (x + 0.044715 * (x * x * x))))


def _gmlp_body(pb_ref, g_ref, ws_ref, bias_ref, o_ref):
    z = _gelu_tanh(pb_ref[0])
    u = z[:, 0:BR_WIDTH]
    v = z[:, BR_WIDTH:2 * BR_WIDTH]
    mu = jnp.mean(v, -1, keepdims=True)
    vc = v - mu
    var = jnp.mean(vc * vc, -1, keepdims=True)
    v = (vc * lax.rsqrt(var + LN_EPS) * g_ref[...]).astype(BF16)
    lane = _iota((GMLP_CHUNK, LANES), 1)
    gw = BR_WIDTH // GMLP_GROUPS
    for c in range(pb_ref.shape[1] // GMLP_CHUNK):
        rows = slice(c * GMLP_CHUNK, (c + 1) * GMLP_CHUNK)
        for half in range(BR_WIDTH // LANES):
            cols = slice(half * LANES, (half + 1) * LANES)
            vch = v[rows, cols]
            a0 = _dot(ws_ref[2 * half], vch)
            a1 = _dot(ws_ref[2 * half + 1], vch)
            mixed = jnp.where(lane < gw, a0, a1) + bias_ref[:, cols]
            o_ref[0, rows, cols] = u[rows, cols] * mixed


def _gmlp(pb, ln_g, w_s, bias):
    b, s, _ = pb.shape
    tm = TOKEN_TILE
    full = lambda a: pl.BlockSpec(a.shape, lambda i, j: (0,) * a.ndim)
    return pl.pallas_call(
        _gmlp_body,
        grid=(b, s // tm),
        in_specs=[pl.BlockSpec((1, tm, PB_COLS), lambda i, j: (i, j, 0)),
                  full(ln_g), full(w_s), full(bias)],
        out_specs=pl.BlockSpec((1, tm, BR_WIDTH), lambda i, j: (i, j, 0)),
        out_shape=jax.ShapeDtypeStruct((b, s, BR_WIDTH), F32),
        compiler_params=_cparams(("arbitrary", "arbitrary")),
        name="gmlp",
    )(pb, ln_g, w_s, bias)


def _pair_stack(a, lane2):
    lo = jnp.where(lane2 < DN_DK, a, jnp.zeros_like(a))
    hi = jnp.where(lane2 >= DN_DK, a, jnp.zeros_like(a))
    return jnp.concatenate([lo, hi], axis=0)


def _pair_blockdiag(a, bdmask):
    return jnp.where(bdmask, jnp.concatenate([a, a], axis=0), 0.0)


def _dn_chunk_body(pd_ref, prev_ref, next_ref, pab_ref, cw_ref, alog_ref, dtb_ref,
                   u_ref, w_ref, qk_ref, qd_ref, kd_ref, el_ref,
                   qkv_s, g_s, beta_s, *, n_ctx_tiles, n_tiles):
    tm = pd_ref.shape[1]
    c = DN_CHUNK
    j = pl.program_id(1)
    x = pd_ref[0]
    has_prev = jnp.logical_and(j != 0, j != n_ctx_tiles)
    has_next = jnp.logical_and(j != n_ctx_tiles - 1, j != n_tiles - 1)
    prev_row = jnp.where(has_prev, prev_ref[0, SUBLANES - 1:SUBLANES, :], 0.0)
    next_row = jnp.where(has_next, next_ref[0, 0:1, :], 0.0)
    row = _iota((tm, 1), 0)
    x_m = jnp.where(row == 0, prev_row, pltpu.roll(x, 1, axis=0))
    x_p = jnp.where(row == tm - 1, next_row, pltpu.roll(x, tm - 1, axis=0))
    y = x_m * cw_ref[0:1, :] + x * cw_ref[1:2, :] + x_p * cw_ref[2:3, :]
    y = y * _sigmoid(y)
    ones_bd = _block_ones(BR_WIDTH, DN_DK)
    q = y[:, 0:BR_WIDTH]
    k = y[:, BR_WIDTH:2 * BR_WIDTH]
    q = q * lax.rsqrt(_dot_f32(q * q, ones_bd) + LN_EPS) * (DN_DK ** -0.5)
    k = k * lax.rsqrt(_dot_f32(k * k, ones_bd) + LN_EPS)
    qkv_s[:, 0:BR_WIDTH] = q
    qkv_s[:, BR_WIDTH:2 * BR_WIDTH] = k
    qkv_s[:, 2 * BR_WIDTH:3 * BR_WIDTH] = y[:, 2 * BR_WIDTH:3 * BR_WIDTH]
    ab = pab_ref[0]
    z = ab + dtb_ref[...]
    softplus = jnp.maximum(z, 0.0) + jnp.log(1.0 + jnp.exp(-jnp.abs(z)))
    g_s[...] = -jnp.exp(alog_ref[...]) * softplus
    beta_s[...] = _sigmoid(ab)

    ii = _iota((c, LANES), 0)
    lane2 = _iota((c, LANES), 1)
    jj = lane2 % c
    eye2 = (ii == jj).astype(F32)
    bdmask = _iota((LANES, LANES), 0) // c == _iota((LANES, LANES), 1) // c
    ri = _iota((c, c), 0)
    ci = _iota((c, c), 1)
    tri_f = (ri >= ci).astype(F32)
    tri_b = (ri <= ci).astype(F32)
    first_half = lane2 < DN_DK
    nh = DN_HEADS

    def chunk_group(grp, carry):
        systems = []
        for cc in range(DN_CHUNKS_PER_GROUP):
            ch = grp * DN_CHUNKS_PER_GROUP + cc
            rows = pl.ds(pl.multiple_of(ch * c, c), c)
            el_rows = pl.ds(pl.multiple_of(ch * SUBLANES, SUBLANES), SUBLANES)
            gc = g_s[rows, :]
            bc = beta_s[rows, :]
            gam = jnp.where(lane2 < nh, _dot_f32(tri_f, gc), _dot_f32(tri_b, gc))
            gam_t = jnp.concatenate([gam, gam], axis=0).T
            for d in range(2):
                incl = (ii >= jj) if d == 0 else (ii <= jj)
                strict = (ii > jj) if d == 0 else (ii < jj)
                for p in range(nh // 2):
                    l0 = d * nh + 2 * p
                    l1 = l0 + 1
                    gcol = jnp.where(first_half, gam[:, l0:l0 + 1], gam[:, l1:l1 + 1])
                    grow = jnp.where(first_half, gam_t[l0:l0 + 1, :], gam_t[l1:l1 + 1, :])
                    bcol = jnp.where(first_half, bc[:, 2 * nh + l0:2 * nh + l0 + 1],
                                     bc[:, 2 * nh + l1:2 * nh + l1 + 1])
                    kg = qkv_s[rows, BR_WIDTH + p * LANES:BR_WIDTH + (p + 1) * LANES]
                    systems.append(dict(
                        d=d, rows=rows, el_rows=el_rows, lanes=slice(p * LANES, (p + 1) * LANES),
                        strict=strict, gcol=gcol, bcol=bcol, kg=kg, kb=kg * bcol,
                        dec=jnp.where(incl, jnp.exp(jnp.where(incl, gcol - grow, 0.0)), 0.0),
                        qg=qkv_s[rows, p * LANES:(p + 1) * LANES],
                        vg=qkv_s[rows, 2 * BR_WIDTH + p * LANES:2 * BR_WIDTH + (p + 1) * LANES],
                        kstack=_pair_stack(kg, lane2).astype(BF16)))
        for sy in systems:
            kk = _dot_nt(sy["kb"].astype(BF16), sy["kstack"])
            sy["pm"] = -jnp.where(sy["strict"], kk * sy["dec"], 0.0)
        for sy in systems:
            qk_ref[0, sy["d"], sy["rows"], sy["lanes"]] = _dot_nt(sy["qg"].astype(BF16), sy["kstack"]) * sy["dec"]
        for sy in systems:
            sy["t"] = eye2 + sy["pm"]
            sy["bd"] = _pair_blockdiag(sy["pm"], bdmask).astype(BF16)
        for _ in range(5):
            for sy in systems:
                sy["pm"] = _dot(sy["pm"].astype(BF16), sy["bd"])
            for sy in systems:
                sy["bd"] = _pair_blockdiag(sy["pm"], bdmask).astype(BF16)
            for sy in systems:
                sy["t"] = sy["t"] + _dot(sy["t"].astype(BF16), sy["bd"])
        for sy in systems:
            sy["eg"] = jnp.exp(sy["gcol"])
            sy["tb"] = sy["t"].astype(BF16)
        for sy in systems:
            u_ref[0, sy["d"], sy["rows"], sy["lanes"]] = _dot(
                sy["tb"], _pair_stack(sy["vg"] * sy["bcol"], lane2).astype(BF16))
        for sy in systems:
            w_ref[0, sy["d"], sy["rows"], sy["lanes"]] = _dot(
                sy["tb"], _pair_stack(sy["kb"] * sy["eg"], lane2).astype(BF16))
        for sy in systems:
            gcol = sy["gcol"]
            glast = gcol[c - 1:c, :] if sy["d"] == 0 else gcol[0:1, :]
            qd_ref[0, sy["d"], sy["rows"], sy["lanes"]] = sy["qg"] * sy["eg"]
            kd_ref[0, sy["d"], sy["rows"], sy["lanes"]] = sy["kg"] * jnp.exp(glast - gcol)
            el_ref[0, sy["d"], sy["el_rows"], sy["lanes"]] = jnp.broadcast_to(jnp.exp(glast), (SUBLANES, LANES))
        return carry

    lax.fori_loop(0, tm // (c * DN_CHUNKS_PER_GROUP), chunk_group, 0)


def _dn_chunk(pd, pab, conv_w, alog, dtb, n_ctx):
    b, s, _ = pd.shape
    tm = TOKEN_TILE
    nt = s // tm
    nsub = tm // SUBLANES
    last_blk = s // SUBLANES - 1
    full = lambda a: pl.BlockSpec(a.shape, lambda i, j: (0,) * a.ndim)
    dir_tok = lambda: pl.BlockSpec((1, 2, tm, BR_WIDTH), lambda i, j: (i, 0, j, 0))
    dir_shape = jax.ShapeDtypeStruct((b, 2, s, BR_WIDTH), F32)
    el_rows = (tm // DN_CHUNK) * SUBLANES
    return pl.pallas_call(
        functools.partial(_dn_chunk_body, n_ctx_tiles=n_ctx // tm, n_tiles=nt),
        grid=(b, nt),
        in_specs=[pl.BlockSpec((1, tm, PD_COLS), lambda i, j: (i, j, 0)),
                  pl.BlockSpec((1, SUBLANES, PD_COLS), lambda i, j: (i, jnp.maximum(j * nsub - 1, 0), 0)),
                  pl.BlockSpec((1, SUBLANES, PD_COLS),
                               lambda i, j: (i, jnp.minimum((j + 1) * nsub, last_blk), 0)),
                  pl.BlockSpec((1, tm, PAB_COLS), lambda i, j: (i, j, 0)),
                  full(conv_w), full(alog), full(dtb)],
        out_specs=[dir_tok(), dir_tok(), dir_tok(), dir_tok(), dir_tok(),
                   pl.BlockSpec((1, 2, el_rows, BR_WIDTH), lambda i, j: (i, 0, j, 0))],
        out_shape=[dir_shape] * 5 + [jax.ShapeDtypeStruct((b, 2, nt * el_rows, BR_WIDTH), F32)],
        scratch_shapes=[pltpu.VMEM((tm, PD_COLS), F32), pltpu.VMEM((tm, PAB_COLS), F32),
                        pltpu.VMEM((tm, PAB_COLS), F32)],
        compiler_params=_cparams(("arbitrary", "arbitrary")),
        name="dn_chunk",
    )(pd, pd, pd, pab, conv_w, alog, dtb)


def _dn_scan_body(uf, wf, qkf, qdf, kdf, elf, ub, wb, qkb, qdb, kdb, elb, of_ref, ob_ref, s_ref):
    tm = uf.shape[2]
    c = DN_CHUNK
    n_chunks = tm // c

    @pl.when(pl.program_id(1) == 0)
    def _():
        s_ref[...] = jnp.zeros_like(s_ref)

    bdmask = _iota((LANES, LANES), 0) // c == _iota((LANES, LANES), 1) // c
    views = ((uf, wf, qkf, qdf, kdf, elf, of_ref), (ub, wb, qkb, qdb, kdb, elb, ob_ref))
    chains = [(d, p) for d in range(2) for p in range(DN_HEADS // 2)]
    state = {dp: s_ref[dp[0], dp[1]] for dp in chains}
    for step in range(n_chunks):
        blk = {}
        for d, p in chains:
            ch = step if d == 0 else n_chunks - 1 - step
            blk[d, p] = (slice(ch * c, (ch + 1) * c), slice(p * LANES, (p + 1) * LANES), ch)
        tile = lambda ref, dp: ref[0, 0, blk[dp][0], blk[dp][1]]
        sb = {dp: state[dp].astype(BF16) for dp in chains}
        ws = {dp: _dot(tile(views[dp[0]][1], dp).astype(BF16), sb[dp]) for dp in chains}
        qs = {dp: _dot(tile(views[dp[0]][3], dp).astype(BF16), sb[dp]) for dp in chains}
        vb = {dp: (tile(views[dp[0]][0], dp) - ws[dp]).astype(BF16) for dp in chains}
        for dp in chains:
            vbd = jnp.where(bdmask, jnp.concatenate([vb[dp], vb[dp]], axis=0), jnp.zeros((), BF16))
            views[dp[0]][6][0, blk[dp][0], blk[dp][1]] = qs[dp] + _dot(tile(views[dp[0]][2], dp).astype(BF16), vbd)
        upd = {dp: _dot_tn(tile(views[dp[0]][4], dp).astype(BF16), vb[dp]) for dp in chains}
        for dp in chains:
            ch = blk[dp][2]
            el = views[dp[0]][5][0, 0, ch * SUBLANES:ch * SUBLANES + 1, blk[dp][1]]
            state[dp] = state[dp] * el + jnp.where(bdmask, upd[dp], 0.0)
    for d, p in chains:
        s_ref[d, p] = state[d, p]


def _dn_scan(u, w, qk, qd, kd, el, n_ctx):
    b, _, s, _ = u.shape
    tm = TOKEN_TILE
    nt = s // tm
    nctx = n_ctx // tm
    el_rows = (tm // DN_CHUNK) * SUBLANES

    def rev(j):
        return jnp.where(j < nctx, nctx - 1 - j, nt - 1 - (j - nctx))

    fwd = lambda rows: pl.BlockSpec((1, 1, rows, BR_WIDTH), lambda i, j: (i, 0, j, 0))
    bwd = lambda rows: pl.BlockSpec((1, 1, rows, BR_WIDTH), lambda i, j: (i, 1, rev(j), 0))
    in_specs = [fwd(tm)] * 5 + [fwd(el_rows)] + [bwd(tm)] * 5 + [bwd(el_rows)]
    out_shape = jax.ShapeDtypeStruct((b, s, BR_WIDTH), F32)
    return pl.pallas_call(
        _dn_scan_body,
        grid=(b, nt),
        in_specs=in_specs,
        out_specs=[pl.BlockSpec((1, tm, BR_WIDTH), lambda i, j: (i, j, 0)),
                   pl.BlockSpec((1, tm, BR_WIDTH), lambda i, j: (i, rev(j), 0))],
        out_shape=[out_shape, out_shape],
        scratch_shapes=[pltpu.VMEM((2, DN_HEADS // 2, LANES, LANES), F32)],
        compiler_params=_cparams(("arbitrary", "arbitrary")),
        name="dn_scan",
    )(u, w, qk, qd, kd, el, u, w, qk, qd, kd, el)


def _merge_body(x_ref, mod_ref, h_ref, pg_ref, ya_ref, yb_ref, yc_ref, of_ref, ob_ref, dng_ref,
                wg_ref, wb_ref, wo_ref, lng_ref, lnb_ref, o_ref, *, alpha):
    hb = h_ref[0]
    o = of_ref[0] + ob_ref[0]
    ms = _dot_f32(o * o, _block_ones(BR_WIDTH, DN_DV)) * (1.0 / DN_DV)
    yd = o * lax.rsqrt(ms + LN_EPS) * dng_ref[...]
    pg = pg_ref[0]
    sg = pg * _sigmoid(pg)
    acc = None
    for i, y in enumerate((ya_ref[0], yb_ref[0], yc_ref[0], yd)):
        t = (y * sg[:, i * BR_WIDTH:(i + 1) * BR_WIDTH]).astype(BF16)
        term = _sigmoid(_dot(hb, wg_ref[i])) * _dot(t, wb_ref[i])
        acc = term if acc is None else acc + term
    out = _dot(acc.astype(BF16), wo_ref[...])
    gt = mod_ref[0, 0, 2:3, :]
    r = alpha * x_ref[0] + gt * out
    mu = jnp.mean(r, -1, keepdims=True)
    rc = r - mu
    var = jnp.mean(rc * rc, -1, keepdims=True)
    o_ref[0] = rc * lax.rsqrt(var + LN_EPS) * lng_ref[...] + lnb_ref[...]


def _merge(xs, mod, h, pg, ya, yb, yc, o_f, o_b, dng, wg, wb, wo, lng, lnb, n_ctx_tiles, skip_tiles, alpha):
    b, s, d = xs.shape
    tm = TOKEN_TILE
    nt = s // tm - skip_tiles
    tok = lambda width: pl.BlockSpec((1, tm, width), lambda i, j: (i, j + skip_tiles, 0))
    full = lambda a: pl.BlockSpec(a.shape, lambda i, j: (0,) * a.ndim)
    return pl.pallas_call(
        functools.partial(_merge_body, alpha=alpha),
        grid=(b, nt),
        in_specs=[tok(d),
                  pl.BlockSpec((1, 1, 3, d),
                               lambda i, j: (i, jnp.where(j + skip_tiles >= n_ctx_tiles, 1, 0), 0, 0)),
                  tok(d), tok(PG_COLS), tok(BR_WIDTH), tok(BR_WIDTH), tok(BR_WIDTH), tok(BR_WIDTH),
                  tok(BR_WIDTH), full(dng), full(wg), full(wb), full(wo), full(lng), full(lnb)],
        out_specs=pl.BlockSpec((1, tm, d), lambda i, j: (i, j, 0)),
        out_shape=jax.ShapeDtypeStruct((b, nt * tm, d), F32),
        compiler_params=_cparams(("arbitrary", "arbitrary")),
        name="merge",
    )(xs, mod, h, pg, ya, yb, yc, o_f, o_b, dng, wg, wb, wo, lng, lnb)


def _rope_swap_index(n_blocks):
    blk = jnp.array(list(range(8, 16)) + list(range(0, 8)) + list(range(24, 32)) + list(range(16, 24)))
    return (jnp.arange(n_blocks)[:, None] * 32 + blk[None, :]).reshape(-1)


def _pack_w_in(w):
    d = w.shape[0]
    o = 0
    cq = w[:, o:o + MLA_Q_LORA]; o += MLA_Q_LORA
    ckv = w[:, o:o + MLA_KV_LORA]; o += MLA_KV_LORA
    kr = w[:, o:o + MLA_ROPE]; o += MLA_ROPE
    pb = w[:, o:o + GMLP_COLS]; o += GMLP_COLS
    dq = w[:, o:o + 256]; dk = w[:, o + 256:o + 512]; dv = w[:, o + 512:o + 768]; o += DIFF_COLS
    dn_qkv = w[:, o:o + 3 * BR_WIDTH]; dn_ab = w[:, o + 3 * BR_WIDTH:o + DN_COLS]; o += DN_COLS
    pg = w[:, o:]
    z = lambda n: jnp.zeros((d, n), w.dtype)
    sw1 = _rope_swap_index(1)
    sw8 = _rope_swap_index(8)
    place = lambda a: jnp.concatenate([z(MLA_NOPE), a, z(LANES - MLA_NOPE - MLA_ROPE)], axis=1)
    packed = jnp.concatenate(
        [cq, ckv, place(kr), place(kr[:, sw1]), pb, dq, dk, dv, dq[:, sw8], dk[:, sw8], dn_qkv,
         dn_ab, z(PAB_COLS - 4 * DN_HEADS), pg], axis=1)
    return packed.astype(BF16)


def _pack_mla_weights(w_uq, w_ukv):
    dq = MLA_NOPE + MLA_ROPE
    wq = w_uq.reshape(MLA_Q_LORA, MLA_HEADS, dq)
    zq = jnp.zeros((MLA_Q_LORA, MLA_HEADS, LANES - dq), w_uq.dtype)
    wq_p = jnp.concatenate([wq, zq], axis=2).reshape(MLA_Q_LORA, MLA_HEADS * LANES)
    rope_sw = wq[:, :, MLA_NOPE:][:, :, _rope_swap_index(1)]
    wqs_p = jnp.concatenate([jnp.zeros((MLA_Q_LORA, MLA_HEADS, MLA_NOPE), w_uq.dtype), rope_sw, zq],
                            axis=2).reshape(MLA_Q_LORA, MLA_HEADS * LANES)
    wkv = w_ukv.reshape(MLA_KV_LORA, MLA_HEADS, MLA_NOPE + MLA_V)
    zk = jnp.zeros((MLA_KV_LORA, MLA_HEADS, LANES - MLA_NOPE), w_ukv.dtype)
    wk_p = jnp.concatenate([wkv[:, :, :MLA_NOPE], zk], axis=2).reshape(MLA_KV_LORA, MLA_HEADS * LANES)
    wv_p = wkv[:, :, MLA_NOPE:].reshape(MLA_KV_LORA, MLA_HEADS * MLA_V)
    return wq_p.astype(BF16), wqs_p.astype(BF16), wk_p.astype(BF16), wv_p.astype(BF16)


def _rope_tables(n, n_ctx):
    rows = n // GRID_W
    row = jnp.repeat(jnp.arange(rows, dtype=F32), GRID_W)
    col = jnp.tile(jnp.arange(GRID_W, dtype=F32), rows)
    axis_dim = MLA_ROPE // 2
    inv_freq = ROPE_BASE ** (-jnp.arange(0, axis_dim, 2, dtype=F32) / axis_dim)
    ar = row[:, None] * inv_freq
    ac = col[:, None] * inv_freq
    cos32 = jnp.concatenate([jnp.cos(ar), jnp.cos(ar), jnp.cos(ac), jnp.cos(ac)], axis=1)
    sin32 = jnp.concatenate([-jnp.sin(ar), jnp.sin(ar), -jnp.sin(ac), jnp.sin(ac)], axis=1)
    cos32 = jnp.concatenate([jnp.ones((n_ctx, 32), F32), cos32], axis=0)
    sin32 = jnp.concatenate([jnp.zeros((n_ctx, 32), F32), sin32], axis=0)
    s = n + n_ctx
    one = lambda w: jnp.ones((s, w), F32)
    zero = lambda w: jnp.zeros((s, w), F32)
    mla_cos = jnp.concatenate([one(MLA_NOPE), cos32, one(LANES - MLA_NOPE - MLA_ROPE)], axis=1)
    mla_sin = jnp.concatenate([zero(MLA_NOPE), sin32, zero(LANES - MLA_NOPE - MLA_ROPE)], axis=1)
    diff_cos = jnp.tile(cos32, (1, LANES // 32))
    diff_sin = jnp.tile(sin32, (1, LANES // 32))
    return mla_cos, mla_sin, diff_cos, diff_sin


def _pad_lanes(a, width=LANES):
    return jnp.concatenate([a, jnp.zeros(a.shape[:-1] + (width - a.shape[-1],), a.dtype)], axis=-1)


def kernel(x, c, ctx, c_ctx, w_mod, b_mod, w_in, mla_q_norm, mla_w_uq, mla_kv_norm, mla_w_ukv, gmlp_ln_g, gmlp_w_s, gmlp_b_s, diff_lq1, diff_lk1, diff_lq2, diff_lk2, diff_norm_g, dn_conv_w, dn_a_log, dn_dt_bias, dn_norm_g, w_gate, w_branch, w_out, ln_g, ln_b):
    b, n, d = x.shape
    n_ctx = ctx.shape[1]
    depth = w_mod.shape[0]
    tm = TOKEN_TILE
    assert d == D_MODEL and n % tm == 0 and n_ctx % tm == 0 and n % GRID_W == 0
    assert tm == ATTN_Q_TILE == ATTN_K_TILE
    alpha = (2 * depth) ** 0.25
    n_ctx_tiles = n_ctx // tm

    xs = jnp.concatenate([ctx, x], axis=1)
    mla_cos, mla_sin, diff_cos, diff_sin = _rope_tables(n, n_ctx)

    rows = ((b + 1 + SUBLANES - 1) // SUBLANES) * SUBLANES
    cc = jnp.concatenate([c, c_ctx[None, :], jnp.zeros((rows - b - 1, d), F32)], axis=0)
    mod_all = _modulation(cc, w_mod, b_mod)

    for l in range(depth):
        last = l == depth - 1
        lam_init = 0.8 - 0.6 * math.exp(-0.3 * l)
        mod_l = mod_all[l].reshape(rows, 3, d)
        mod = jnp.stack([jnp.broadcast_to(mod_l[b][None], (b, 3, d)), mod_l[:b]], axis=1)

        h, pa, pb, pc, pd, pab, pg = _inproj(xs, mod, _pack_w_in(w_in[l]), n_ctx_tiles)

        wq, wqs, wk, wv = _pack_mla_weights(mla_w_uq[l], mla_w_ukv[l])
        q_a, k_a, vt_a = _mla_prep(pa, mla_cos, mla_sin, mla_q_norm[l][None, :], mla_kv_norm[l][None, :],
                                   wq, wqs, wk, wv)
        ya = _mla_attn(q_a, k_a, vt_a, n_ctx)

        bias = jnp.repeat(gmlp_b_s[l].T, BR_WIDTH // GMLP_GROUPS, axis=1)
        yb = _gmlp(pb, gmlp_ln_g[l][None, :], gmlp_w_s[l].astype(BF16), bias)

        q_c, k_c, vt_c = _diff_prep(pc, diff_cos, diff_sin)
        lqk = _pad_lanes(jnp.stack([diff_lq1[l], diff_lk1[l], diff_lq2[l], diff_lk2[l]], axis=0))
        yc = _diff_attn(q_c, k_c, vt_c, lqk, diff_norm_g[l][:, None], n_ctx, lam_init)

        conv_w = jnp.concatenate([dn_conv_w[l], jnp.zeros((SUBLANES - 3, PD_COLS), F32)], axis=0)
        alog = _pad_lanes(dn_a_log[l].reshape(1, 2 * DN_HEADS))
        dtb = _pad_lanes(dn_dt_bias[l].reshape(1, 2 * DN_HEADS))
        u, w, qk, qd, kd, el = _dn_chunk(pd, pab, conv_w, alog, dtb, n_ctx)
        o_f, o_b = _dn_scan(u, w, qk, qd, kd, el, n_ctx)

        dng = jnp.tile(dn_norm_g[l], DN_HEADS)[None, :]
        xs = _merge(xs, mod, h, pg, ya, yb, yc, o_f, o_b, dng, w_gate[l].astype(BF16),
                    w_branch[l].astype(BF16), w_out[l].astype(BF16), ln_g[l][None, :], ln_b[l][None, :],
                    n_ctx_tiles, n_ctx_tiles if last else 0, alpha)
    return xs
```

```python
import functools
import math

import jax
import jax.numpy as jnp
from jax import lax
from jax.experimental import pallas as pl
from jax.experimental.pallas import tpu as pltpu

F32 = jnp.float32
BF16 = jnp.bfloat16

D_MODEL = 1024
GRID_W = 64
N_BRANCH = 4
BR_WIDTH = 256
MLA_HEADS = 4
MLA_NOPE = 64
MLA_ROPE = 32
MLA_V = 64
MLA_Q_LORA = 256
MLA_KV_LORA = 128
GMLP_GROUPS = 4
GMLP_CHUNK = 128
DIFF_HEADS = 4
DIFF_D = 32
DN_HEADS = 4
DN_DK = 64
DN_DV = 64
DN_CHUNK = 64
ROPE_BASE = 10000.0
LN_EPS = 1e-6

MLA_COLS = MLA_Q_LORA + MLA_KV_LORA + MLA_ROPE
GMLP_COLS = 2 * BR_WIDTH
DIFF_COLS = 3 * DIFF_HEADS * 2 * DIFF_D
DN_COLS = 3 * BR_WIDTH + 4 * DN_HEADS

LANES = 128
SUBLANES = 8
VMEM_LIMIT_BYTES = 56 * 1024 * 1024

TOKEN_TILE = 256
ATTN_Q_TILE = 256
ATTN_K_TILE = 256
DN_CHUNKS_PER_GROUP = 4
ATTN_STEPS_PER_TRIP = 64
ATTN_LOOKAHEAD = 5
NEG_BIG = -1e30
VT_ROWS = 80
LOG2_E = 1.4426950408889634

PA_COLS = 640
PB_COLS = 512
PC_COLS = 1280
PD_COLS = 768
PAB_COLS = 128
PG_COLS = 1024
PACK_SPLITS = (PA_COLS, PB_COLS, PC_COLS, PD_COLS, PAB_COLS, PG_COLS)
PACK_COLS = sum(PACK_SPLITS)


def _cparams(semantics):
    return pltpu.CompilerParams(dimension_semantics=semantics, vmem_limit_bytes=VMEM_LIMIT_BYTES)


def _dot(a, b):
    return jnp.dot(a, b, preferred_element_type=F32)


def _dot_nt(a, b):
    return lax.dot_general(a, b, (((1,), (1,)), ((), ())), preferred_element_type=F32)


def _dot_tn(a, b):
    return lax.dot_general(a, b, (((0,), (0,)), ((), ())), preferred_element_type=F32)


def _split_bf16(x, terms):
    pieces = []
    for _ in range(terms):
        hi = x.astype(BF16)
        pieces.append(hi)
        x = x - hi.astype(F32)
    return pieces


def _sum_dot(x, sel, terms=3):
    return sum(_dot(p, sel) for p in _split_bf16(x, terms))


def _sel_dot(sel, x, terms=3):
    return sum(_dot(sel, p) for p in _split_bf16(x, terms))


def _sigmoid(x):
    return 1.0 / (1.0 + jnp.exp(-x))


def _iota(shape, axis):
    return lax.broadcasted_iota(jnp.int32, shape, axis)


def _block_ones(n, blk):
    return (_iota((n, n), 0) // blk == _iota((n, n), 1) // blk).astype(F32)


def _mod_body(c_ref, w_ref, b_ref, o_ref):
    c = c_ref[...]
    s = (c * _sigmoid(c)).astype(BF16)
    o_ref[0] = _dot(s, w_ref[0].astype(BF16)) + b_ref[0]


def _modulation(cc, w_mod, b_mod):
    depth, d, n3 = w_mod.shape
    rows = cc.shape[0]
    tn = 1024
    return pl.pallas_call(
        _mod_body,
        grid=(depth, n3 // tn),
        in_specs=[pl.BlockSpec((rows, d), lambda l, j: (0, 0)),
                  pl.BlockSpec((1, d, tn), lambda l, j: (l, 0, j)),
                  pl.BlockSpec((1, 1, tn), lambda l, j: (l, 0, j))],
        out_specs=pl.BlockSpec((1, rows, tn), lambda l, j: (l, 0, j)),
        out_shape=jax.ShapeDtypeStruct((depth, rows, n3), F32),
        compiler_params=_cparams(("arbitrary", "arbitrary")),
        name="modulation",
    )(cc, w_mod, b_mod.reshape(depth, 1, n3))


def _rms(x, g):
    return x * lax.rsqrt(jnp.mean(x * x, -1, keepdims=True) + LN_EPS) * g


def _mla_prep_values(pa, cos, sin, qn, kvn, wq, wqs, wk, wv, scale):
    cq = pa[:, 0:MLA_Q_LORA]
    ckv = pa[:, MLA_Q_LORA:MLA_Q_LORA + MLA_KV_LORA]
    kr = pa[:, 384:512]
    kr_sw = pa[:, 512:640]
    cos4 = jnp.concatenate([cos] * MLA_HEADS, axis=1)
    sin4 = jnp.concatenate([sin] * MLA_HEADS, axis=1)
    cqn = _rms(cq, qn).astype(BF16)
    q = ((_dot(cqn, wq) * cos4 + _dot(cqn, wqs) * sin4) * scale).astype(BF16)
    ckvn = _rms(ckv, kvn).astype(BF16)
    kr_rot = kr * cos + kr_sw * sin
    k = (_dot(ckvn, wk) + jnp.concatenate([kr_rot] * MLA_HEADS, axis=1)).astype(BF16)
    return q, k, _vt_with_ones(_dot(ckvn, wv), MLA_HEADS, MLA_V)


def _diff_prep_values(pc, cos, sin, scale):
    cos = jnp.concatenate([cos] * 2, axis=1)
    sin = jnp.concatenate([sin] * 2, axis=1)
    q = ((pc[:, 0:256] * cos + pc[:, 768:1024] * sin) * scale).astype(BF16)
    k = (pc[:, 256:512] * cos + pc[:, 1024:1280] * sin).astype(BF16)
    return q, k, _vt_with_ones(pc[:, 512:768], DIFF_HEADS, 2 * DIFF_D)


def _gelu_tanh(x):
    return 0.5 * x * (1.0 + jnp.tanh(math.sqrt(2.0 / math.pi) * (x + 0.044715 * (x * x * x))))


def _gmlp_values(pb, g, ws_ref, bias_ref, o_ref):
    z = _gelu_tanh(pb)
    u = z[:, 0:BR_WIDTH]
    v = z[:, BR_WIDTH:2 * BR_WIDTH]
    mu = jnp.mean(v, -1, keepdims=True)
    vc = v - mu
    var = jnp.mean(vc * vc, -1, keepdims=True)
    v = (vc * lax.rsqrt(var + LN_EPS) * g).astype(BF16)
    lane = _iota((GMLP_CHUNK, LANES), 1)
    gw = BR_WIDTH // GMLP_GROUPS
    for c in range(pb.shape[0] // GMLP_CHUNK):
        rows = slice(c * GMLP_CHUNK, (c + 1) * GMLP_CHUNK)
        for half in range(BR_WIDTH // LANES):
            cols = slice(half * LANES, (half + 1) * LANES)
            vch = v[rows, cols]
            a0 = _dot(ws_ref[2 * half], vch)
            a1 = _dot(ws_ref[2 * half + 1], vch)
            mixed = jnp.where(lane < gw, a0, a1) + bias_ref[:, cols]
            o_ref[0, rows, cols] = u[rows, cols] * mixed


def _inproj_body(x_ref, mod_ref, w_ref, cosm_ref, sinm_ref, cosd_ref, sind_ref, qn_ref, kvn_ref,
                 wq_ref, wqs_ref, wk_ref, wv_ref, gg_ref, ws_ref, bias_ref,
                 h_ref, pd_ref, pab_ref, pg_ref, qa_ref, ka_ref, vta_ref, yb_ref, qc_ref, kc_ref, vtc_ref,
                 *, mla_scale, diff_scale):
    x = x_ref[0]
    mu = jnp.mean(x, -1, keepdims=True)
    xc = x - mu
    var = jnp.mean(xc * xc, -1, keepdims=True)
    xn = xc * lax.rsqrt(var + LN_EPS)
    sh = mod_ref[0, 0, 0:1, :]
    sc = mod_ref[0, 0, 1:2, :]
    hb = (xn * (1.0 + sc) + sh).astype(BF16)
    h_ref[0] = hb
    offs = [sum(PACK_SPLITS[:i]) for i in range(len(PACK_SPLITS))]
    proj = lambda i: _dot(hb, w_ref[:, offs[i]:offs[i] + PACK_SPLITS[i]])
    qa_ref[0], ka_ref[0], vta_ref[0, 0] = _mla_prep_values(
        proj(0), cosm_ref[...], sinm_ref[...], qn_ref[...], kvn_ref[...], wq_ref[...], wqs_ref[...],
        wk_ref[...], wv_ref[...], mla_scale)
    _gmlp_values(proj(1), gg_ref[...], ws_ref, bias_ref, yb_ref)
    qc_ref[0], kc_ref[0], vtc_ref[0, 0] = _diff_prep_values(proj(2), cosd_ref[...], sind_ref[...], diff_scale)
    pd_ref[0] = proj(3)
    pab_ref[0] = proj(4)
    pg_ref[0] = proj(5)


def _inproj(xs, mod, w_pack, tables, mla_params, gmlp_params, n_ctx_tiles):
    b, s, d = xs.shape
    tm = TOKEN_TILE
    hp = MLA_HEADS * LANES
    tok = lambda width, dt: jax.ShapeDtypeStruct((b, s, width), dt)
    tok_spec = lambda width: pl.BlockSpec((1, tm, width), lambda i, j: (i, j, 0))
    full = lambda a: pl.BlockSpec(a.shape, lambda i, j: (0,) * a.ndim)
    table = pl.BlockSpec((tm, LANES), lambda i, j: (j, 0))
    vt_spec = pl.BlockSpec((1, 1, 4 * VT_ROWS, tm), lambda i, j: (i, j, 0, 0))
    vt_shape = jax.ShapeDtypeStruct((b, s // tm, 4 * VT_ROWS, tm), BF16)
    consts = list(mla_params) + list(gmlp_params)
    return pl.pallas_call(
        functools.partial(_inproj_body, mla_scale=(MLA_NOPE + MLA_ROPE) ** -0.5 * LOG2_E,
                          diff_scale=DIFF_D ** -0.5 * LOG2_E),
        grid=(b, s // tm),
        in_specs=[tok_spec(d),
                  pl.BlockSpec((1, 1, 3, d), lambda i, j: (i, jnp.where(j >= n_ctx_tiles, 1, 0), 0, 0)),
                  pl.BlockSpec((d, PACK_COLS), lambda i, j: (0, 0))]
        + [table] * 4 + [full(a) for a in consts],
        out_specs=[tok_spec(d), tok_spec(PD_COLS), tok_spec(PAB_COLS), tok_spec(PG_COLS),
                   tok_spec(hp), tok_spec(hp), vt_spec, tok_spec(BR_WIDTH),
                   tok_spec(BR_WIDTH), tok_spec(BR_WIDTH), vt_spec],
        out_shape=[tok(d, BF16), tok(PD_COLS, F32), tok(PAB_COLS, F32), tok(PG_COLS, F32),
                   tok(hp, BF16), tok(hp, BF16), vt_shape, tok(BR_WIDTH, F32),
                   tok(BR_WIDTH, BF16), tok(BR_WIDTH, BF16), vt_shape],
        compiler_params=_cparams(("arbitrary", "arbitrary")),
        name="inproj",
    )(xs, mod, w_pack, *tables, *consts)


def _vt_with_ones(v, heads, dv):
    tm = v.shape[0]
    vt = v.T
    aug = (_iota((VT_ROWS - dv, tm), 0) == 0).astype(F32)
    pieces = []
    for h in range(heads):
        pieces += [vt[h * dv:(h + 1) * dv], aug]
    return jnp.concatenate(pieces, axis=0).astype(BF16)


def _tiles_per_iter(n_tiles, n_chain):
    t = max(1, ATTN_STEPS_PER_TRIP // n_chain)
    while n_tiles % t:
        t -= 1
    return t


def _flash_chains(qt_s, k_ref, vt_ref, k_lanes, v_rows, n_k, n_first, tiles_per_iter, m_s, acc_s):
    tk = ATTN_K_TILE
    m_s[...] = jnp.full(m_s.shape, NEG_BIG, F32)
    acc_s[...] = jnp.zeros(acc_s.shape, F32)

    n_chain = len(k_lanes)

    def scores(c, kt):
        k_rows = pl.ds(pl.multiple_of(kt * tk, tk), tk)
        return _dot(k_ref[0, k_rows, k_lanes[c][0]:k_lanes[c][1]], qt_s[c])

    def run(kt0, n_iter, tiles_per_iter):
        per_iter = tiles_per_iter * n_chain

        def step(it, ahead):
            ahead = list(ahead)
            base = kt0 + it * tiles_per_iter
            for idx in range(per_iter):
                kt, c = base + idx // n_chain, idx % n_chain
                s = ahead.pop(0)
                nxt = idx + ATTN_LOOKAHEAD
                ahead.append(scores(nxt % n_chain, jnp.minimum(base + nxt // n_chain, n_k - 1)))
                m_old = m_s[c]
                m_new = jnp.maximum(m_old, jnp.max(s, axis=0, keepdims=True))
                p = jnp.exp2(s - m_new).astype(BF16)
                vt = vt_ref[0, kt, v_rows[c][0]:v_rows[c][1], :]
                acc_s[c] = jnp.exp2(m_old - m_new) * acc_s[c] + _dot(vt, p)
                m_s[c] = m_new
            return tuple(ahead)

        init = tuple(scores(i % n_chain, jnp.minimum(kt0 + i // n_chain, n_k - 1))
                     for i in range(ATTN_LOOKAHEAD))
        lax.fori_loop(0, n_iter, step, init)

    run(0, n_first, 1)
    run(n_first, (n_k - n_first) // tiles_per_iter, tiles_per_iter)


def _mla_attn_body(q_ref, k_ref, vt_ref, o_ref, qt_s, m_s, acc_s, *, n_ctx_q_tiles, n_ctx_k_tiles,
                   n_k_tiles):
    j = pl.program_id(1)
    n_k = jnp.where(j < n_ctx_q_tiles, n_ctx_k_tiles, n_k_tiles)
    k_lanes = [(h * LANES, (h + 1) * LANES) for h in range(MLA_HEADS)]
    v_rows = [(h * VT_ROWS, (h + 1) * VT_ROWS) for h in range(MLA_HEADS)]
    for h in range(MLA_HEADS):
        qt_s[h] = q_ref[0, :, h * LANES:(h + 1) * LANES].astype(F32).T.astype(BF16)
    _flash_chains(qt_s, k_ref, vt_ref, k_lanes, v_rows, n_k, n_ctx_k_tiles,
                  _tiles_per_iter(n_k_tiles - n_ctx_k_tiles, MLA_HEADS), m_s, acc_s)
    outs = []
    for h in range(MLA_HEADS):
        acc = acc_s[h]
        outs.append(acc[0:MLA_V] / acc[MLA_V:MLA_V + 1])
    o_ref[0] = jnp.concatenate(outs, axis=0).T


def _mla_attn(q, k, vt, n_ctx):
    b, s, hp = q.shape
    tq, tk = ATTN_Q_TILE, ATTN_K_TILE
    return pl.pallas_call(
        functools.partial(_mla_attn_body, n_ctx_q_tiles=n_ctx // tq, n_ctx_k_tiles=n_ctx // tk,
                          n_k_tiles=s // tk),
        grid=(b, s // tq),
        in_specs=[pl.BlockSpec((1, tq, hp), lambda i, j: (i, j, 0)),
                  pl.BlockSpec((1, s, hp), lambda i, j: (i, 0, 0)),
                  pl.BlockSpec((1, s // tk, MLA_HEADS * VT_ROWS, tk), lambda i, j: (i, 0, 0, 0))],
        out_specs=pl.BlockSpec((1, tq, BR_WIDTH), lambda i, j: (i, j, 0)),
        out_shape=jax.ShapeDtypeStruct((b, s, BR_WIDTH), F32),
        scratch_shapes=[pltpu.VMEM((MLA_HEADS, LANES, tq), BF16), pltpu.VMEM((MLA_HEADS, 1, tq), F32),
                        pltpu.VMEM((MLA_HEADS, VT_ROWS, tq), F32)],
        compiler_params=_cparams(("arbitrary", "arbitrary")),
        name="mla_attn",
    )(q, k, vt)


def _diff_attn_body(q_ref, k_ref, vt_ref, lqk_ref, g_ref, o_ref, qm_s, m_s, acc_s, *, n_ctx_q_tiles,
                    n_ctx_k_tiles, n_k_tiles, lam_init):
    tq = q_ref.shape[1]
    j = pl.program_id(1)
    n_k = jnp.where(j < n_ctx_q_tiles, n_ctx_k_tiles, n_k_tiles)
    lqk = lqk_ref[...]
    lam = (jnp.exp(jnp.sum(lqk[0:1] * lqk[1:2], axis=1, keepdims=True))
           - jnp.exp(jnp.sum(lqk[2:3] * lqk[3:4], axis=1, keepdims=True)) + lam_init)
    row = _iota((LANES, tq), 0)
    dv = 2 * DIFF_D
    k_lanes, v_rows = [], []
    for grp in range(2 * DIFF_HEADS * DIFF_D // LANES):
        qgt = q_ref[0, :, grp * LANES:(grp + 1) * LANES].astype(F32).T
        for sub in range(LANES // DIFF_D):
            blk = grp * (LANES // DIFF_D) + sub
            keep = (row >= sub * DIFF_D) & (row < (sub + 1) * DIFF_D)
            qm_s[blk] = jnp.where(keep, qgt, 0.0).astype(BF16)
            k_lanes.append((grp * LANES, (grp + 1) * LANES))
            v_rows.append(((blk // 2) * VT_ROWS, (blk // 2 + 1) * VT_ROWS))
    _flash_chains(qm_s, k_ref, vt_ref, k_lanes, v_rows, n_k, n_ctx_k_tiles,
                  _tiles_per_iter(n_k_tiles - n_ctx_k_tiles, 2 * DIFF_HEADS), m_s, acc_s)
    outs = []
    for h in range(DIFF_HEADS):
        a1 = acc_s[2 * h]
        a2 = acc_s[2 * h + 1]
        o = a1[0:dv] / a1[dv:dv + 1] - lam * (a2[0:dv] / a2[dv:dv + 1])
        o = o * lax.rsqrt(jnp.mean(o * o, axis=0, keepdims=True) + LN_EPS)
        outs.append(o * g_ref[...] * (1.0 - lam_init))
    o_ref[0] = jnp.concatenate(outs, axis=0).T


def _diff_attn(q, k, vt, lqk, g_col, n_ctx, lam_init):
    b, s, _ = q.shape
    tq, tk = ATTN_Q_TILE, ATTN_K_TILE
    return pl.pallas_call(
        functools.partial(_diff_attn_body, n_ctx_q_tiles=n_ctx // tq, n_ctx_k_tiles=n_ctx // tk,
                          n_k_tiles=s // tk, lam_init=lam_init),
        grid=(b, s // tq),
        in_specs=[pl.BlockSpec((1, tq, BR_WIDTH), lambda i, j: (i, j, 0)),
                  pl.BlockSpec((1, s, BR_WIDTH), lambda i, j: (i, 0, 0)),
                  pl.BlockSpec((1, s // tk, DIFF_HEADS * VT_ROWS, tk), lambda i, j: (i, 0, 0, 0)),
                  pl.BlockSpec(lqk.shape, lambda i, j: (0, 0)),
                  pl.BlockSpec(g_col.shape, lambda i, j: (0, 0))],
        out_specs=pl.BlockSpec((1, tq, BR_WIDTH), lambda i, j: (i, j, 0)),
        out_shape=jax.ShapeDtypeStruct((b, s, BR_WIDTH), F32),
        scratch_shapes=[pltpu.VMEM((2 * DIFF_HEADS, LANES, tq), BF16),
                        pltpu.VMEM((2 * DIFF_HEADS, 1, tq), F32),
                        pltpu.VMEM((2 * DIFF_HEADS, VT_ROWS, tq), F32)],
        compiler_params=_cparams(("arbitrary", "arbitrary")),
        name="diff_attn",
    )(q, k, vt, lqk, g_col)


def _pair_stack(a, lane2):
    lo = jnp.where(lane2 < DN_DK, a, jnp.zeros_like(a))
    hi = jnp.where(lane2 >= DN_DK, a, jnp.zeros_like(a))
    return jnp.concatenate([lo, hi], axis=0)


def _pair_blockdiag(a, bdmask):
    return jnp.where(bdmask, jnp.concatenate([a, a], axis=0), 0.0)


def _dn_chunk_body(pd_ref, prev_ref, next_ref, pab_ref, cw_ref, alog_ref, dtb_ref,
                   u_ref, w_ref, qk_ref, qd_ref, kd_ref, el_ref,
                   qkv_s, g_s, beta_s, *, n_ctx_tiles, n_tiles):
    tm = pd_ref.shape[1]
    c = DN_CHUNK
    j = pl.program_id(1)
    x = pd_ref[0]
    has_prev = jnp.logical_and(j != 0, j != n_ctx_tiles)
    has_next = jnp.logical_and(j != n_ctx_tiles - 1, j != n_tiles - 1)
    prev_row = jnp.where(has_prev, prev_ref[0, SUBLANES - 1:SUBLANES, :], 0.0)
    next_row = jnp.where(has_next, next_ref[0, 0:1, :], 0.0)
    row = _iota((tm, 1), 0)
    x_m = jnp.where(row == 0, prev_row, pltpu.roll(x, 1, axis=0))
    x_p = jnp.where(row == tm - 1, next_row, pltpu.roll(x, tm - 1, axis=0))
    y = x_m * cw_ref[0:1, :] + x * cw_ref[1:2, :] + x_p * cw_ref[2:3, :]
    y = y * _sigmoid(y)
    ones_bd = _block_ones(BR_WIDTH, DN_DK).astype(BF16)
    q = y[:, 0:BR_WIDTH]
    k = y[:, BR_WIDTH:2 * BR_WIDTH]
    q = q * lax.rsqrt(_sum_dot(q * q, ones_bd, 2) + LN_EPS) * (DN_DK ** -0.5)
    k = k * lax.rsqrt(_sum_dot(k * k, ones_bd, 2) + LN_EPS)
    qkv_s[:, 0:BR_WIDTH] = q
    qkv_s[:, BR_WIDTH:2 * BR_WIDTH] = k
    qkv_s[:, 2 * BR_WIDTH:3 * BR_WIDTH] = y[:, 2 * BR_WIDTH:3 * BR_WIDTH]
    ab = pab_ref[0]
    z = ab + dtb_ref[...]
    softplus = jnp.maximum(z, 0.0) + jnp.log(1.0 + jnp.exp(-jnp.abs(z)))
    g_s[...] = -jnp.exp(alog_ref[...]) * softplus
    beta_s[...] = _sigmoid(ab)

    ii = _iota((c, LANES), 0)
    lane2 = _iota((c, LANES), 1)
    jj = lane2 % c
    eye2 = (ii == jj).astype(F32)
    bdmask = _iota((LANES, LANES), 0) // c == _iota((LANES, LANES), 1) // c
    ri = _iota((c, c), 0)
    ci = _iota((c, c), 1)
    tri_f = (ri >= ci).astype(F32).astype(BF16)
    tri_b = (ri <= ci).astype(F32).astype(BF16)
    first_half = lane2 < DN_DK
    nh = DN_HEADS

    def chunk_group(grp, carry):
        systems = []
        for cc in range(DN_CHUNKS_PER_GROUP):
            ch = grp * DN_CHUNKS_PER_GROUP + cc
            rows = pl.ds(pl.multiple_of(ch * c, c), c)
            el_rows = pl.ds(pl.multiple_of(ch * SUBLANES, SUBLANES), SUBLANES)
            gc = g_s[rows, :]
            bc = beta_s[rows, :]
            gam = jnp.where(lane2 < nh, _sel_dot(tri_f, gc), _sel_dot(tri_b, gc))
            gam_t = jnp.concatenate([gam, gam], axis=0).T
            for d in range(2):
                incl = (ii >= jj) if d == 0 else (ii <= jj)
                strict = (ii > jj) if d == 0 else (ii < jj)
                for p in range(nh // 2):
                    l0 = d * nh + 2 * p
                    l1 = l0 + 1
                    gcol = jnp.where(first_half, gam[:, l0:l0 + 1], gam[:, l1:l1 + 1])
                    grow = jnp.where(first_half, gam_t[l0:l0 + 1, :], gam_t[l1:l1 + 1, :])
                    bcol = jnp.where(first_half, bc[:, 2 * nh + l0:2 * nh + l0 + 1],
                                     bc[:, 2 * nh + l1:2 * nh + l1 + 1])
                    kg = qkv_s[rows, BR_WIDTH + p * LANES:BR_WIDTH + (p + 1) * LANES]
                    systems.append(dict(
                        d=d, rows=rows, el_rows=el_rows, lanes=slice(p * LANES, (p + 1) * LANES),
                        strict=strict, gcol=gcol, bcol=bcol, kg=kg, kb=kg * bcol,
                        dec=jnp.where(incl, jnp.exp(jnp.where(incl, gcol - grow, 0.0)), 0.0),
                        qg=qkv_s[rows, p * LANES:(p + 1) * LANES],
                        vg=qkv_s[rows, 2 * BR_WIDTH + p * LANES:2 * BR_WIDTH + (p + 1) * LANES],
                        kstack=_pair_stack(kg, lane2).astype(BF16)))
        for sy in systems:
            kk = _dot_nt(sy["kb"].astype(BF16), sy["kstack"])
            sy["pm"] = -jnp.where(sy["strict"], kk * sy["dec"], 0.0)
        for sy in systems:
            qk_ref[0, sy["d"], sy["rows"], sy["lanes"]] = (
                _dot_nt(sy["qg"].astype(BF16), sy["kstack"]) * sy["dec"]).astype(BF16)
        for sy in systems:
            sy["t"] = eye2 + sy["pm"]
            sy["bd"] = _pair_blockdiag(sy["pm"], bdmask).astype(BF16)
        for _ in range(5):
            for sy in systems:
                sy["pm"] = _dot(sy["pm"].astype(BF16), sy["bd"])
            for sy in systems:
                sy["bd"] = _pair_blockdiag(sy["pm"], bdmask).astype(BF16)
            for sy in systems:
                sy["t"] = sy["t"] + _dot(sy["t"].astype(BF16), sy["bd"])
        for sy in systems:
            sy["eg"] = jnp.exp(sy["gcol"])
            sy["tb"] = sy["t"].astype(BF16)
        for sy in systems:
            u_ref[0, sy["d"], sy["rows"], sy["lanes"]] = _dot(
                sy["tb"], _pair_stack(sy["vg"] * sy["bcol"], lane2).astype(BF16))
        for sy in systems:
            w_ref[0, sy["d"], sy["rows"], sy["lanes"]] = _dot(
                sy["tb"], _pair_stack(sy["kb"] * sy["eg"], lane2).astype(BF16)).astype(BF16)
        for sy in systems:
            gcol = sy["gcol"]
            glast = gcol[c - 1:c, :] if sy["d"] == 0 else gcol[0:1, :]
            qd_ref[0, sy["d"], sy["rows"], sy["lanes"]] = (sy["qg"] * sy["eg"]).astype(BF16)
            kd_ref[0, sy["d"], sy["rows"], sy["lanes"]] = (sy["kg"] * jnp.exp(glast - gcol)).astype(BF16)
            el_ref[0, sy["d"], sy["el_rows"], sy["lanes"]] = jnp.broadcast_to(jnp.exp(glast), (SUBLANES, LANES))
        return carry

    lax.fori_loop(0, tm // (c * DN_CHUNKS_PER_GROUP), chunk_group, 0)


def _dn_chunk(pd, pab, conv_w, alog, dtb, n_ctx):
    b, s, _ = pd.shape
    tm = TOKEN_TILE
    nt = s // tm
    nsub = tm // SUBLANES
    last_blk = s // SUBLANES - 1
    full = lambda a: pl.BlockSpec(a.shape, lambda i, j: (0,) * a.ndim)
    dir_tok = lambda: pl.BlockSpec((1, 2, tm, BR_WIDTH), lambda i, j: (i, 0, j, 0))
    dir_shape = lambda dt: jax.ShapeDtypeStruct((b, 2, s, BR_WIDTH), dt)
    el_rows = (tm // DN_CHUNK) * SUBLANES
    return pl.pallas_call(
        functools.partial(_dn_chunk_body, n_ctx_tiles=n_ctx // tm, n_tiles=nt),
        grid=(b, nt),
        in_specs=[pl.BlockSpec((1, tm, PD_COLS), lambda i, j: (i, j, 0)),
                  pl.BlockSpec((1, SUBLANES, PD_COLS), lambda i, j: (i, jnp.maximum(j * nsub - 1, 0), 0)),
                  pl.BlockSpec((1, SUBLANES, PD_COLS),
                               lambda i, j: (i, jnp.minimum((j + 1) * nsub, last_blk), 0)),
                  pl.BlockSpec((1, tm, PAB_COLS), lambda i, j: (i, j, 0)),
                  full(conv_w), full(alog), full(dtb)],
        out_specs=[dir_tok(), dir_tok(), dir_tok(), dir_tok(), dir_tok(),
                   pl.BlockSpec((1, 2, el_rows, BR_WIDTH), lambda i, j: (i, 0, j, 0))],
        out_shape=[dir_shape(F32)] + [dir_shape(BF16)] * 4
        + [jax.ShapeDtypeStruct((b, 2, nt * el_rows, BR_WIDTH), F32)],
        scratch_shapes=[pltpu.VMEM((tm, PD_COLS), F32), pltpu.VMEM((tm, PAB_COLS), F32),
                        pltpu.VMEM((tm, PAB_COLS), F32)],
        compiler_params=_cparams(("arbitrary", "arbitrary")),
        name="dn_chunk",
    )(pd, pd, pd, pab, conv_w, alog, dtb)


def _dn_scan_body(uf, wf, qkf, qdf, kdf, elf, ub, wb, qkb, qdb, kdb, elb, of_ref, ob_ref, s_ref):
    tm = uf.shape[2]
    c = DN_CHUNK
    n_chunks = tm // c

    @pl.when(pl.program_id(1) == 0)
    def _():
        s_ref[...] = jnp.zeros_like(s_ref)

    bdmask = _iota((LANES, LANES), 0) // c == _iota((LANES, LANES), 1) // c
    views = ((uf, wf, qkf, qdf, kdf, elf, of_ref), (ub, wb, qkb, qdb, kdb, elb, ob_ref))
    chains = [(d, p) for d in range(2) for p in range(DN_HEADS // 2)]
    state = {dp: s_ref[dp[0], dp[1]] for dp in chains}
    for step in range(n_chunks):
        blk = {}
        for d, p in chains:
            ch = step if d == 0 else n_chunks - 1 - step
            blk[d, p] = (slice(ch * c, (ch + 1) * c), slice(p * LANES, (p + 1) * LANES), ch)
        tile = lambda ref, dp: ref[0, 0, blk[dp][0], blk[dp][1]]
        sb = {dp: state[dp].astype(BF16) for dp in chains}
        ws = {dp: _dot(tile(views[dp[0]][1], dp), sb[dp]) for dp in chains}
        qs = {dp: _dot(tile(views[dp[0]][3], dp), sb[dp]) for dp in chains}
        vb = {dp: (tile(views[dp[0]][0], dp) - ws[dp]).astype(BF16) for dp in chains}
        for dp in chains:
            vbd = jnp.where(bdmask, jnp.concatenate([vb[dp], vb[dp]], axis=0), jnp.zeros((), BF16))
            views[dp[0]][6][0, blk[dp][0], blk[dp][1]] = qs[dp] + _dot(tile(views[dp[0]][2], dp), vbd)
        upd = {dp: _dot_tn(tile(views[dp[0]][4], dp), vb[dp]) for dp in chains}
        for dp in chains:
            ch = blk[dp][2]
            el = views[dp[0]][5][0, 0, ch * SUBLANES:ch * SUBLANES + 1, blk[dp][1]]
            state[dp] = state[dp] * el + jnp.where(bdmask, upd[dp], 0.0)
    for d, p in chains:
        s_ref[d, p] = state[d, p]


def _dn_scan(u, w, qk, qd, kd, el, n_ctx):
    b, _, s, _ = u.shape
    tm = TOKEN_TILE
    nt = s // tm
    nctx = n_ctx // tm
    el_rows = (tm // DN_CHUNK) * SUBLANES

    def rev(j):
        return jnp.where(j < nctx, nctx - 1 - j, nt - 1 - (j - nctx))

    fwd = lambda rows: pl.BlockSpec((1, 1, rows, BR_WIDTH), lambda i, j: (i, 0, j, 0))
    bwd = lambda rows: pl.BlockSpec((1, 1, rows, BR_WIDTH), lambda i, j: (i, 1, rev(j), 0))
    in_specs = [fwd(tm)] * 5 + [fwd(el_rows)] + [bwd(tm)] * 5 + [bwd(el_rows)]
    out_shape = jax.ShapeDtypeStruct((b, s, BR_WIDTH), F32)
    return pl.pallas_call(
        _dn_scan_body,
        grid=(b, nt),
        in_specs=in_specs,
        out_specs=[pl.BlockSpec((1, tm, BR_WIDTH), lambda i, j: (i, j, 0)),
                   pl.BlockSpec((1, tm, BR_WIDTH), lambda i, j: (i, rev(j), 0))],
        out_shape=[out_shape, out_shape],
        scratch_shapes=[pltpu.VMEM((2, DN_HEADS // 2, LANES, LANES), F32)],
        compiler_params=_cparams(("arbitrary", "arbitrary")),
        name="dn_scan",
    )(u, w, qk, qd, kd, el, u, w, qk, qd, kd, el)


def _merge_body(x_ref, mod_ref, h_ref, pg_ref, ya_ref, yb_ref, yc_ref, of_ref, ob_ref, dng_ref,
                wg_ref, wb_ref, wo_ref, lng_ref, lnb_ref, o_ref, *, alpha):
    hb = h_ref[0]
    o = of_ref[0] + ob_ref[0]
    ms = _sum_dot(o * o, _block_ones(BR_WIDTH, DN_DV).astype(BF16), 2) * (1.0 / DN_DV)
    yd = o * lax.rsqrt(ms + LN_EPS) * dng_ref[...]
    pg = pg_ref[0]
    sg = pg * _sigmoid(pg)
    acc = None
    for i, y in enumerate((ya_ref[0], yb_ref[0], yc_ref[0], yd)):
        t = (y * sg[:, i * BR_WIDTH:(i + 1) * BR_WIDTH]).astype(BF16)
        term = _sigmoid(_dot(hb, wg_ref[i])) * _dot(t, wb_ref[i])
        acc = term if acc is None else acc + term
    out = _dot(acc.astype(BF16), wo_ref[...])
    gt = mod_ref[0, 0, 2:3, :]
    r = alpha * x_ref[0] + gt * out
    mu = jnp.mean(r, -1, keepdims=True)
    rc = r - mu
    var = jnp.mean(rc * rc, -1, keepdims=True)
    o_ref[0] = rc * lax.rsqrt(var + LN_EPS) * lng_ref[...] + lnb_ref[...]


def _merge(xs, mod, h, pg, ya, yb, yc, o_f, o_b, dng, wg, wb, wo, lng, lnb, n_ctx_tiles, skip_tiles, alpha):
    b, s, d = xs.shape
    tm = TOKEN_TILE
    nt = s // tm - skip_tiles
    tok = lambda width: pl.BlockSpec((1, tm, width), lambda i, j: (i, j + skip_tiles, 0))
    full = lambda a: pl.BlockSpec(a.shape, lambda i, j: (0,) * a.ndim)
    return pl.pallas_call(
        functools.partial(_merge_body, alpha=alpha),
        grid=(b, nt),
        in_specs=[tok(d),
                  pl.BlockSpec((1, 1, 3, d),
                               lambda i, j: (i, jnp.where(j + skip_tiles >= n_ctx_tiles, 1, 0), 0, 0)),
                  tok(d), tok(PG_COLS), tok(BR_WIDTH), tok(BR_WIDTH), tok(BR_WIDTH), tok(BR_WIDTH),
                  tok(BR_WIDTH), full(dng), full(wg), full(wb), full(wo), full(lng), full(lnb)],
        out_specs=pl.BlockSpec((1, tm, d), lambda i, j: (i, j, 0)),
        out_shape=jax.ShapeDtypeStruct((b, nt * tm, d), F32),
        compiler_params=_cparams(("arbitrary", "arbitrary")),
        name="merge",
    )(xs, mod, h, pg, ya, yb, yc, o_f, o_b, dng, wg, wb, wo, lng, lnb)


def _rope_swap_index(n_blocks):
    blk = jnp.array(list(range(8, 16)) + list(range(0, 8)) + list(range(24, 32)) + list(range(16, 24)))
    return (jnp.arange(n_blocks)[:, None] * 32 + blk[None, :]).reshape(-1)


def _pack_w_in(w):
    d = w.shape[0]
    o = 0
    cq = w[:, o:o + MLA_Q_LORA]; o += MLA_Q_LORA
    ckv = w[:, o:o + MLA_KV_LORA]; o += MLA_KV_LORA
    kr = w[:, o:o + MLA_ROPE]; o += MLA_ROPE
    pb = w[:, o:o + GMLP_COLS]; o += GMLP_COLS
    dq = w[:, o:o + 256]; dk = w[:, o + 256:o + 512]; dv = w[:, o + 512:o + 768]; o += DIFF_COLS
    dn_qkv = w[:, o:o + 3 * BR_WIDTH]; dn_ab = w[:, o + 3 * BR_WIDTH:o + DN_COLS]; o += DN_COLS
    pg = w[:, o:]
    z = lambda n: jnp.zeros((d, n), w.dtype)
    sw1 = _rope_swap_index(1)
    sw8 = _rope_swap_index(8)
    place = lambda a: jnp.concatenate([z(MLA_NOPE), a, z(LANES - MLA_NOPE - MLA_ROPE)], axis=1)
    packed = jnp.concatenate(
        [cq, ckv, place(kr), place(kr[:, sw1]), pb, dq, dk, dv, dq[:, sw8], dk[:, sw8], dn_qkv,
         dn_ab, z(PAB_COLS - 4 * DN_HEADS), pg], axis=1)
    return packed.astype(BF16)


def _pack_mla_weights(w_uq, w_ukv):
    dq = MLA_NOPE + MLA_ROPE
    wq = w_uq.reshape(MLA_Q_LORA, MLA_HEADS, dq)
    zq = jnp.zeros((MLA_Q_LORA, MLA_HEADS, LANES - dq), w_uq.dtype)
    wq_p = jnp.concatenate([wq, zq], axis=2).reshape(MLA_Q_LORA, MLA_HEADS * LANES)
    rope_sw = wq[:, :, MLA_NOPE:][:, :, _rope_swap_index(1)]
    wqs_p = jnp.concatenate([jnp.zeros((MLA_Q_LORA, MLA_HEADS, MLA_NOPE), w_uq.dtype), rope_sw, zq],
                            axis=2).reshape(MLA_Q_LORA, MLA_HEADS * LANES)
    wkv = w_ukv.reshape(MLA_KV_LORA, MLA_HEADS, MLA_NOPE + MLA_V)
    zk = jnp.zeros((MLA_KV_LORA, MLA_HEADS, LANES - MLA_NOPE), w_ukv.dtype)
    wk_p = jnp.concatenate([wkv[:, :, :MLA_NOPE], zk], axis=2).reshape(MLA_KV_LORA, MLA_HEADS * LANES)
    wv_p = wkv[:, :, MLA_NOPE:].reshape(MLA_KV_LORA, MLA_HEADS * MLA_V)
    return wq_p.astype(BF16), wqs_p.astype(BF16), wk_p.astype(BF16), wv_p.astype(BF16)


def _rope_tables(n, n_ctx):
    rows = n // GRID_W
    row = jnp.repeat(jnp.arange(rows, dtype=F32), GRID_W)
    col = jnp.tile(jnp.arange(GRID_W, dtype=F32), rows)
    axis_dim = MLA_ROPE // 2
    inv_freq = ROPE_BASE ** (-jnp.arange(0, axis_dim, 2, dtype=F32) / axis_dim)
    ar = row[:, None] * inv_freq
    ac = col[:, None] * inv_freq
    cos32 = jnp.concatenate([jnp.cos(ar), jnp.cos(ar), jnp.cos(ac), jnp.cos(ac)], axis=1)
    sin32 = jnp.concatenate([-jnp.sin(ar), jnp.sin(ar), -jnp.sin(ac), jnp.sin(ac)], axis=1)
    cos32 = jnp.concatenate([jnp.ones((n_ctx, 32), F32), cos32], axis=0)
    sin32 = jnp.concatenate([jnp.zeros((n_ctx, 32), F32), sin32], axis=0)
    s = n + n_ctx
    one = lambda w: jnp.ones((s, w), F32)
    zero = lambda w: jnp.zeros((s, w), F32)
    mla_cos = jnp.concatenate([one(MLA_NOPE), cos32, one(LANES - MLA_NOPE - MLA_ROPE)], axis=1)
    mla_sin = jnp.concatenate([zero(MLA_NOPE), sin32, zero(LANES - MLA_NOPE - MLA_ROPE)], axis=1)
    diff_cos = jnp.tile(cos32, (1, LANES // 32))
    diff_sin = jnp.tile(sin32, (1, LANES // 32))
    return mla_cos, mla_sin, diff_cos, diff_sin


def _pad_lanes(a, width=LANES):
    return jnp.concatenate([a, jnp.zeros(a.shape[:-1] + (width - a.shape[-1],), a.dtype)], axis=-1)


def kernel(x, c, ctx, c_ctx, w_mod, b_mod, w_in, mla_q_norm, mla_w_uq, mla_kv_norm, mla_w_ukv, gmlp_ln_g, gmlp_w_s, gmlp_b_s, diff_lq1, diff_lk1, diff_lq2, diff_lk2, diff_norm_g, dn_conv_w, dn_a_log, dn_dt_bias, dn_norm_g, w_gate, w_branch, w_out, ln_g, ln_b):
    b, n, d = x.shape
    n_ctx = ctx.shape[1]
    depth = w_mod.shape[0]
    tm = TOKEN_TILE
    assert d == D_MODEL and n % tm == 0 and n_ctx % tm == 0 and n % GRID_W == 0
    assert tm == ATTN_Q_TILE == ATTN_K_TILE
    alpha = (2 * depth) ** 0.25
    n_ctx_tiles = n_ctx // tm

    xs = jnp.concatenate([ctx, x], axis=1)
    mla_cos, mla_sin, diff_cos, diff_sin = _rope_tables(n, n_ctx)

    rows = ((b + 1 + SUBLANES - 1) // SUBLANES) * SUBLANES
    cc = jnp.concatenate([c, c_ctx[None, :], jnp.zeros((rows - b - 1, d), F32)], axis=0)
    mod_all = _modulation(cc, w_mod, b_mod)

    for l in range(depth):
        last = l == depth - 1
        lam_init = 0.8 - 0.6 * math.exp(-0.3 * l)
        mod_l = mod_all[l].reshape(rows, 3, d)
        mod = jnp.stack([jnp.broadcast_to(mod_l[b][None], (b, 3, d)), mod_l[:b]], axis=1)

        wq, wqs, wk, wv = _pack_mla_weights(mla_w_uq[l], mla_w_ukv[l])
        bias = jnp.repeat(gmlp_b_s[l].T, BR_WIDTH // GMLP_GROUPS, axis=1)
        h, pd, pab, pg, q_a, k_a, vt_a, yb, q_c, k_c, vt_c = _inproj(
            xs, mod, _pack_w_in(w_in[l]), (mla_cos, mla_sin, diff_cos, diff_sin),
            (mla_q_norm[l][None, :], mla_kv_norm[l][None, :], wq, wqs, wk, wv),
            (gmlp_ln_g[l][None, :], gmlp_w_s[l].astype(BF16), bias), n_ctx_tiles)
        ya = _mla_attn(q_a, k_a, vt_a, n_ctx)
        lqk = _pad_lanes(jnp.stack([diff_lq1[l], diff_lk1[l], diff_lq2[l], diff_lk2[l]], axis=0))
        yc = _diff_attn(q_c, k_c, vt_c, lqk, diff_norm_g[l][:, None], n_ctx, lam_init)

        conv_w = jnp.concatenate([dn_conv_w[l], jnp.zeros((SUBLANES - 3, PD_COLS), F32)], axis=0)
        alog = _pad_lanes(dn_a_log[l].reshape(1, 2 * DN_HEADS))
        dtb = _pad_lanes(dn_dt_bias[l].reshape(1, 2 * DN_HEADS))
        u, w, qk, qd, kd, el = _dn_chunk(pd, pab, conv_w, alog, dtb, n_ctx)
        o_f, o_b = _dn_scan(u, w, qk, qd, kd, el, n_ctx)

        dng = jnp.tile(dn_norm_g[l], DN_HEADS)[None, :]
        xs = _merge(xs, mod, h, pg, ya, yb, yc, o_f, o_b, dng, w_gate[l].astype(BF16),
                    w_branch[l].astype(BF16), w_out[l].astype(BF16), ln_g[l][None, :], ln_b[l][None, :],
                    n_ctx_tiles, n_ctx_tiles if last else 0, alpha)
    return xs
```

```python
import functools
import math

import jax
import jax.numpy as jnp
from jax import lax
from jax.experimental import pallas as pl
from jax.experimental.pallas import tpu as pltpu

F32 = jnp.float32
BF16 = jnp.bfloat16

D_MODEL = 1024
GRID_W = 64
N_BRANCH = 4
BR_WIDTH = 256
MLA_HEADS = 4
MLA_NOPE = 64
MLA_ROPE = 32
MLA_V = 64
MLA_Q_LORA = 256
MLA_KV_LORA = 128
GMLP_GROUPS = 4
GMLP_CHUNK = 128
DIFF_HEADS = 4
DIFF_D = 32
DN_HEADS = 4
DN_DK = 64
DN_DV = 64
DN_CHUNK = 64
ROPE_BASE = 10000.0
LN_EPS = 1e-6

MLA_COLS = MLA_Q_LORA + MLA_KV_LORA + MLA_ROPE
GMLP_COLS = 2 * BR_WIDTH
DIFF_COLS = 3 * DIFF_HEADS * 2 * DIFF_D
DN_COLS = 3 * BR_WIDTH + 4 * DN_HEADS

LANES = 128
SUBLANES = 8
VMEM_LIMIT_BYTES = 56 * 1024 * 1024

TOKEN_TILE = 256
ATTN_Q_TILE = 256
ATTN_K_TILE = 256
DN_CHUNKS_PER_GROUP = 4
DN_BATCH_PER_STEP = 2
ATTN_STEPS_PER_TRIP = 64
ATTN_LOOKAHEAD = 5
NEG_BIG = -1e30
VT_ROWS = 80
LOG2_E = 1.4426950408889634

PA_COLS = 640
PB_COLS = 512
PC_COLS = 1280
PD_COLS = 768
PAB_COLS = 128
PG_COLS = 1024
PACK_SPLITS = (PA_COLS, PB_COLS, PC_COLS, PD_COLS, PAB_COLS, PG_COLS)
PACK_COLS = sum(PACK_SPLITS)


def _cparams(semantics):
    return pltpu.CompilerParams(dimension_semantics=semantics, vmem_limit_bytes=VMEM_LIMIT_BYTES)


def _dot(a, b):
    return jnp.dot(a, b, preferred_element_type=F32)


def _dot_nt(a, b):
    return lax.dot_general(a, b, (((1,), (1,)), ((), ())), preferred_element_type=F32)


def _dot_tn(a, b):
    return lax.dot_general(a, b, (((0,), (0,)), ((), ())), preferred_element_type=F32)


def _split_bf16(x, terms):
    pieces = []
    for _ in range(terms):
        hi = x.astype(BF16)
        pieces.append(hi)
        x = x - hi.astype(F32)
    return pieces


def _sum_dot(x, sel, terms=3):
    return sum(_dot(p, sel) for p in _split_bf16(x, terms))


def _sel_dot(sel, x, terms=3):
    return sum(_dot(sel, p) for p in _split_bf16(x, terms))


def _sigmoid(x):
    return 1.0 / (1.0 + jnp.exp(-x))


def _iota(shape, axis):
    return lax.broadcasted_iota(jnp.int32, shape, axis)


def _block_ones(n, blk):
    return (_iota((n, n), 0) // blk == _iota((n, n), 1) // blk).astype(F32)


def _mod_body(c_ref, w_ref, b_ref, o_ref):
    c = c_ref[...]
    s = (c * _sigmoid(c)).astype(BF16)
    o_ref[0] = _dot(s, w_ref[0].astype(BF16)) + b_ref[0]


def _modulation(cc, w_mod, b_mod):
    depth, d, n3 = w_mod.shape
    rows = cc.shape[0]
    tn = 1024
    return pl.pallas_call(
        _mod_body,
        grid=(depth, n3 // tn),
        in_specs=[pl.BlockSpec((rows, d), lambda l, j: (0, 0)),
                  pl.BlockSpec((1, d, tn), lambda l, j: (l, 0, j)),
                  pl.BlockSpec((1, 1, tn), lambda l, j: (l, 0, j))],
        out_specs=pl.BlockSpec((1, rows, tn), lambda l, j: (l, 0, j)),
        out_shape=jax.ShapeDtypeStruct((depth, rows, n3), F32),
        compiler_params=_cparams(("arbitrary", "arbitrary")),
        name="modulation",
    )(cc, w_mod, b_mod.reshape(depth, 1, n3))


def _rms(x, g):
    return x * lax.rsqrt(jnp.mean(x * x, -1, keepdims=True) + LN_EPS) * g


def _mla_prep_values(pa, cos, sin, qn, kvn, wq, wqs, wk, wv, scale):
    cq = pa[:, 0:MLA_Q_LORA]
    ckv = pa[:, MLA_Q_LORA:MLA_Q_LORA + MLA_KV_LORA]
    kr = pa[:, 384:512]
    kr_sw = pa[:, 512:640]
    cos4 = jnp.concatenate([cos] * MLA_HEADS, axis=1)
    sin4 = jnp.concatenate([sin] * MLA_HEADS, axis=1)
    cqn = _rms(cq, qn).astype(BF16)
    q = ((_dot(cqn, wq) * cos4 + _dot(cqn, wqs) * sin4) * scale).astype(BF16)
    ckvn = _rms(ckv, kvn).astype(BF16)
    kr_rot = kr * cos + kr_sw * sin
    k = (_dot(ckvn, wk) + jnp.concatenate([kr_rot] * MLA_HEADS, axis=1)).astype(BF16)
    return q, k, _vt_with_ones(_dot(ckvn, wv), MLA_HEADS, MLA_V)


def _diff_prep_values(pc, cos, sin, scale):
    cos = jnp.concatenate([cos] * 2, axis=1)
    sin = jnp.concatenate([sin] * 2, axis=1)
    q = ((pc[:, 0:256] * cos + pc[:, 768:1024] * sin) * scale).astype(BF16)
    k = (pc[:, 256:512] * cos + pc[:, 1024:1280] * sin).astype(BF16)
    return q, k, _vt_with_ones(pc[:, 512:768], DIFF_HEADS, 2 * DIFF_D)


def _gelu_tanh(x):
    return 0.5 * x * (1.0 + jnp.tanh(math.sqrt(2.0 / math.pi) * (x + 0.044715 * (x * x * x))))


def _gmlp_values(pb, g, ws_ref, bias_ref, o_ref):
    z = _gelu_tanh(pb)
    u = z[:, 0:BR_WIDTH]
    v = z[:, BR_WIDTH:2 * BR_WIDTH]
    mu = jnp.mean(v, -1, keepdims=True)
    vc = v - mu
    var = jnp.mean(vc * vc, -1, keepdims=True)
    v = (vc * lax.rsqrt(var + LN_EPS) * g).astype(BF16)
    lane = _iota((GMLP_CHUNK, LANES), 1)
    gw = BR_WIDTH // GMLP_GROUPS
    for c in range(pb.shape[0] // GMLP_CHUNK):
        rows = slice(c * GMLP_CHUNK, (c + 1) * GMLP_CHUNK)
        for half in range(BR_WIDTH // LANES):
            cols = slice(half * LANES, (half + 1) * LANES)
            vch = v[rows, cols]
            a0 = _dot(ws_ref[2 * half], vch)
            a1 = _dot(ws_ref[2 * half + 1], vch)
            mixed = jnp.where(lane < gw, a0, a1) + bias_ref[:, cols]
            o_ref[0, rows, cols] = u[rows, cols] * mixed


def _inproj_body(x_ref, mod_ref, w_ref, cosm_ref, sinm_ref, cosd_ref, sind_ref, qn_ref, kvn_ref,
                 wq_ref, wqs_ref, wk_ref, wv_ref, gg_ref, ws_ref, bias_ref,
                 h_ref, pd_ref, pab_ref, pg_ref, qa_ref, ka_ref, vta_ref, yb_ref, qc_ref, kc_ref, vtc_ref,
                 *, mla_scale, diff_scale):
    x = x_ref[0]
    mu = jnp.mean(x, -1, keepdims=True)
    xc = x - mu
    var = jnp.mean(xc * xc, -1, keepdims=True)
    xn = xc * lax.rsqrt(var + LN_EPS)
    sh = mod_ref[0, 0, 0:1, :]
    sc = mod_ref[0, 0, 1:2, :]
    hb = (xn * (1.0 + sc) + sh).astype(BF16)
    h_ref[0] = hb
    offs = [sum(PACK_SPLITS[:i]) for i in range(len(PACK_SPLITS))]
    proj = lambda i: _dot(hb, w_ref[:, offs[i]:offs[i] + PACK_SPLITS[i]])
    qa_ref[0], ka_ref[0], vta_ref[0, 0] = _mla_prep_values(
        proj(0), cosm_ref[...], sinm_ref[...], qn_ref[...], kvn_ref[...], wq_ref[...], wqs_ref[...],
        wk_ref[...], wv_ref[...], mla_scale)
    _gmlp_values(proj(1), gg_ref[...], ws_ref, bias_ref, yb_ref)
    qc_ref[0], kc_ref[0], vtc_ref[0, 0] = _diff_prep_values(proj(2), cosd_ref[...], sind_ref[...], diff_scale)
    pd_ref[0] = proj(3)
    pab_ref[0] = proj(4)
    pg_ref[0] = proj(5)


def _inproj(xs, mod, w_pack, tables, mla_params, gmlp_params, n_ctx_tiles):
    b, s, d = xs.shape
    tm = TOKEN_TILE
    hp = MLA_HEADS * LANES
    tok = lambda width, dt: jax.ShapeDtypeStruct((b, s, width), dt)
    tok_spec = lambda width: pl.BlockSpec((1, tm, width), lambda i, j: (i, j, 0))
    full = lambda a: pl.BlockSpec(a.shape, lambda i, j: (0,) * a.ndim)
    table = pl.BlockSpec((tm, LANES), lambda i, j: (j, 0))
    vt_spec = pl.BlockSpec((1, 1, 4 * VT_ROWS, tm), lambda i, j: (i, j, 0, 0))
    vt_shape = jax.ShapeDtypeStruct((b, s // tm, 4 * VT_ROWS, tm), BF16)
    consts = list(mla_params) + list(gmlp_params)
    return pl.pallas_call(
        functools.partial(_inproj_body, mla_scale=(MLA_NOPE + MLA_ROPE) ** -0.5 * LOG2_E,
                          diff_scale=DIFF_D ** -0.5 * LOG2_E),
        grid=(b, s // tm),
        in_specs=[tok_spec(d),
                  pl.BlockSpec((1, 1, 3, d), lambda i, j: (i, jnp.where(j >= n_ctx_tiles, 1, 0), 0, 0)),
                  pl.BlockSpec((d, PACK_COLS), lambda i, j: (0, 0))]
        + [table] * 4 + [full(a) for a in consts],
        out_specs=[tok_spec(d), tok_spec(PD_COLS), tok_spec(PAB_COLS), tok_spec(PG_COLS),
                   tok_spec(hp), tok_spec(hp), vt_spec, tok_spec(BR_WIDTH),
                   tok_spec(BR_WIDTH), tok_spec(BR_WIDTH), vt_spec],
        out_shape=[tok(d, BF16), tok(PD_COLS, F32), tok(PAB_COLS, F32), tok(PG_COLS, F32),
                   tok(hp, BF16), tok(hp, BF16), vt_shape, tok(BR_WIDTH, F32),
                   tok(BR_WIDTH, BF16), tok(BR_WIDTH, BF16), vt_shape],
        compiler_params=_cparams(("arbitrary", "arbitrary")),
        name="inproj",
    )(xs, mod, w_pack, *tables, *consts)


def _vt_with_ones(v, heads, dv):
    tm = v.shape[0]
    vt = v.T
    aug = (_iota((VT_ROWS - dv, tm), 0) == 0).astype(F32)
    pieces = []
    for h in range(heads):
        pieces += [vt[h * dv:(h + 1) * dv], aug]
    return jnp.concatenate(pieces, axis=0).astype(BF16)


def _tiles_per_iter(n_tiles, n_chain):
    t = max(1, ATTN_STEPS_PER_TRIP // n_chain)
    while n_tiles % t:
        t -= 1
    return t


def _flash_chains(qt_s, k_ref, vt_ref, k_lanes, v_rows, n_k, n_first, tiles_per_iter, m_s, acc_s):
    tk = ATTN_K_TILE
    m_s[...] = jnp.full(m_s.shape, NEG_BIG, F32)
    acc_s[...] = jnp.zeros(acc_s.shape, F32)

    n_chain = len(k_lanes)

    def scores(c, kt):
        k_rows = pl.ds(pl.multiple_of(kt * tk, tk), tk)
        return _dot(k_ref[0, k_rows, k_lanes[c][0]:k_lanes[c][1]], qt_s[c])

    def run(kt0, n_iter, tiles_per_iter):
        per_iter = tiles_per_iter * n_chain

        def step(it, ahead):
            ahead = list(ahead)
            base = kt0 + it * tiles_per_iter
            for idx in range(per_iter):
                kt, c = base + idx // n_chain, idx % n_chain
                s = ahead.pop(0)
                nxt = idx + ATTN_LOOKAHEAD
                ahead.append(scores(nxt % n_chain, jnp.minimum(base + nxt // n_chain, n_k - 1)))
                m_old = m_s[c]
                m_new = jnp.maximum(m_old, jnp.max(s, axis=0, keepdims=True))
                p = jnp.exp2(s - m_new).astype(BF16)
                vt = vt_ref[0, kt, v_rows[c][0]:v_rows[c][1], :]
                acc_s[c] = jnp.exp2(m_old - m_new) * acc_s[c] + _dot(vt, p)
                m_s[c] = m_new
            return tuple(ahead)

        init = tuple(scores(i % n_chain, jnp.minimum(kt0 + i // n_chain, n_k - 1))
                     for i in range(ATTN_LOOKAHEAD))
        lax.fori_loop(0, n_iter, step, init)

    run(0, n_first, 1)
    run(n_first, (n_k - n_first) // tiles_per_iter, tiles_per_iter)


def _mla_attn_body(q_ref, k_ref, vt_ref, o_ref, qt_s, m_s, acc_s, *, n_ctx_q_tiles, n_ctx_k_tiles,
                   n_k_tiles):
    j = pl.program_id(1)
    n_k = jnp.where(j < n_ctx_q_tiles, n_ctx_k_tiles, n_k_tiles)
    k_lanes = [(h * LANES, (h + 1) * LANES) for h in range(MLA_HEADS)]
    v_rows = [(h * VT_ROWS, (h + 1) * VT_ROWS) for h in range(MLA_HEADS)]
    for h in range(MLA_HEADS):
        qt_s[h] = q_ref[0, :, h * LANES:(h + 1) * LANES].astype(F32).T.astype(BF16)
    _flash_chains(qt_s, k_ref, vt_ref, k_lanes, v_rows, n_k, n_ctx_k_tiles,
                  _tiles_per_iter(n_k_tiles - n_ctx_k_tiles, MLA_HEADS), m_s, acc_s)
    outs = []
    for h in range(MLA_HEADS):
        acc = acc_s[h]
        outs.append(acc[0:MLA_V] / acc[MLA_V:MLA_V + 1])
    o_ref[0] = jnp.concatenate(outs, axis=0).T


def _mla_attn(q, k, vt, n_ctx):
    b, s, hp = q.shape
    tq, tk = ATTN_Q_TILE, ATTN_K_TILE
    return pl.pallas_call(
        functools.partial(_mla_attn_body, n_ctx_q_tiles=n_ctx // tq, n_ctx_k_tiles=n_ctx // tk,
                          n_k_tiles=s // tk),
        grid=(b, s // tq),
        in_specs=[pl.BlockSpec((1, tq, hp), lambda i, j: (i, j, 0)),
                  pl.BlockSpec((1, s, hp), lambda i, j: (i, 0, 0)),
                  pl.BlockSpec((1, s // tk, MLA_HEADS * VT_ROWS, tk), lambda i, j: (i, 0, 0, 0))],
        out_specs=pl.BlockSpec((1, tq, BR_WIDTH), lambda i, j: (i, j, 0)),
        out_shape=jax.ShapeDtypeStruct((b, s, BR_WIDTH), F32),
        scratch_shapes=[pltpu.VMEM((MLA_HEADS, LANES, tq), BF16), pltpu.VMEM((MLA_HEADS, 1, tq), F32),
                        pltpu.VMEM((MLA_HEADS, VT_ROWS, tq), F32)],
        compiler_params=_cparams(("arbitrary", "arbitrary")),
        name="mla_attn",
    )(q, k, vt)


def _diff_attn_body(q_ref, k_ref, vt_ref, lqk_ref, g_ref, o_ref, qm_s, m_s, acc_s, *, n_ctx_q_tiles,
                    n_ctx_k_tiles, n_k_tiles, lam_init):
    tq = q_ref.shape[1]
    j = pl.program_id(1)
    n_k = jnp.where(j < n_ctx_q_tiles, n_ctx_k_tiles, n_k_tiles)
    lqk = lqk_ref[...]
    lam = (jnp.exp(jnp.sum(lqk[0:1] * lqk[1:2], axis=1, keepdims=True))
           - jnp.exp(jnp.sum(lqk[2:3] * lqk[3:4], axis=1, keepdims=True)) + lam_init)
    row = _iota((LANES, tq), 0)
    dv = 2 * DIFF_D
    k_lanes, v_rows = [], []
    for grp in range(2 * DIFF_HEADS * DIFF_D // LANES):
        qgt = q_ref[0, :, grp * LANES:(grp + 1) * LANES].astype(F32).T
        for sub in range(LANES // DIFF_D):
            blk = grp * (LANES // DIFF_D) + sub
            keep = (row >= sub * DIFF_D) & (row < (sub + 1) * DIFF_D)
            qm_s[blk] = jnp.where(keep, qgt, 0.0).astype(BF16)
            k_lanes.append((grp * LANES, (grp + 1) * LANES))
            v_rows.append(((blk // 2) * VT_ROWS, (blk // 2 + 1) * VT_ROWS))
    _flash_chains(qm_s, k_ref, vt_ref, k_lanes, v_rows, n_k, n_ctx_k_tiles,
                  _tiles_per_iter(n_k_tiles - n_ctx_k_tiles, 2 * DIFF_HEADS), m_s, acc_s)
    outs = []
    for h in range(DIFF_HEADS):
        a1 = acc_s[2 * h]
        a2 = acc_s[2 * h + 1]
        o = a1[0:dv] / a1[dv:dv + 1] - lam * (a2[0:dv] / a2[dv:dv + 1])
        o = o * lax.rsqrt(jnp.mean(o * o, axis=0, keepdims=True) + LN_EPS)
        outs.append(o * g_ref[...] * (1.0 - lam_init))
    o_ref[0] = jnp.concatenate(outs, axis=0).T


def _diff_attn(q, k, vt, lqk, g_col, n_ctx, lam_init):
    b, s, _ = q.shape
    tq, tk = ATTN_Q_TILE, ATTN_K_TILE
    return pl.pallas_call(
        functools.partial(_diff_attn_body, n_ctx_q_tiles=n_ctx // tq, n_ctx_k_tiles=n_ctx // tk,
                          n_k_tiles=s // tk, lam_init=lam_init),
        grid=(b, s // tq),
        in_specs=[pl.BlockSpec((1, tq, BR_WIDTH), lambda i, j: (i, j, 0)),
                  pl.BlockSpec((1, s, BR_WIDTH), lambda i, j: (i, 0, 0)),
                  pl.BlockSpec((1, s // tk, DIFF_HEADS * VT_ROWS, tk), lambda i, j: (i, 0, 0, 0)),
                  pl.BlockSpec(lqk.shape, lambda i, j: (0, 0)),
                  pl.BlockSpec(g_col.shape, lambda i, j: (0, 0))],
        out_specs=pl.BlockSpec((1, tq, BR_WIDTH), lambda i, j: (i, j, 0)),
        out_shape=jax.ShapeDtypeStruct((b, s, BR_WIDTH), F32),
        scratch_shapes=[pltpu.VMEM((2 * DIFF_HEADS, LANES, tq), BF16),
                        pltpu.VMEM((2 * DIFF_HEADS, 1, tq), F32),
                        pltpu.VMEM((2 * DIFF_HEADS, VT_ROWS, tq), F32)],
        compiler_params=_cparams(("arbitrary", "arbitrary")),
        name="diff_attn",
    )(q, k, vt, lqk, g_col)


def _pair_stack(a, lane2):
    lo = jnp.where(lane2 < DN_DK, a, jnp.zeros_like(a))
    hi = jnp.where(lane2 >= DN_DK, a, jnp.zeros_like(a))
    return jnp.concatenate([lo, hi], axis=0)


def _pair_blockdiag(a, bdmask):
    return jnp.where(bdmask, jnp.concatenate([a, a], axis=0), 0.0)


def _dn_batch_per_step(b):
    return DN_BATCH_PER_STEP if b % DN_BATCH_PER_STEP == 0 else 1


def _dn_chunk_body(pd_ref, prev_ref, next_ref, pab_ref, cw_ref, alog_ref, dtb_ref,
                   u_ref, w_ref, qk_ref, qd_ref, kd_ref, el_ref,
                   qkv_s, g_s, beta_s, *, n_ctx_tiles, n_tiles):
    nb, tm = pd_ref.shape[0], pd_ref.shape[1]
    c = DN_CHUNK
    j = pl.program_id(1)
    has_prev = jnp.logical_and(j != 0, j != n_ctx_tiles)
    has_next = jnp.logical_and(j != n_ctx_tiles - 1, j != n_tiles - 1)
    row = _iota((tm, 1), 0)
    ones_bd = _block_ones(BR_WIDTH, DN_DK).astype(BF16)
    for bb in range(nb):
        x = pd_ref[bb]
        prev_row = jnp.where(has_prev, prev_ref[bb, SUBLANES - 1:SUBLANES, :], 0.0)
        next_row = jnp.where(has_next, next_ref[bb, 0:1, :], 0.0)
        x_m = jnp.where(row == 0, prev_row, pltpu.roll(x, 1, axis=0))
        x_p = jnp.where(row == tm - 1, next_row, pltpu.roll(x, tm - 1, axis=0))
        y = x_m * cw_ref[0:1, :] + x * cw_ref[1:2, :] + x_p * cw_ref[2:3, :]
        y = y * _sigmoid(y)
        q = y[:, 0:BR_WIDTH]
        k = y[:, BR_WIDTH:2 * BR_WIDTH]
        q = q * lax.rsqrt(_sum_dot(q * q, ones_bd, 2) + LN_EPS) * (DN_DK ** -0.5)
        k = k * lax.rsqrt(_sum_dot(k * k, ones_bd, 2) + LN_EPS)
        qkv_s[bb, :, 0:BR_WIDTH] = q
        qkv_s[bb, :, BR_WIDTH:2 * BR_WIDTH] = k
        qkv_s[bb, :, 2 * BR_WIDTH:3 * BR_WIDTH] = y[:, 2 * BR_WIDTH:3 * BR_WIDTH]
        ab = pab_ref[bb]
        z = ab + dtb_ref[...]
        softplus = jnp.maximum(z, 0.0) + jnp.log(1.0 + jnp.exp(-jnp.abs(z)))
        g_s[bb] = -jnp.exp(alog_ref[...]) * softplus
        beta_s[bb] = _sigmoid(ab)

    ii = _iota((c, LANES), 0)
    lane2 = _iota((c, LANES), 1)
    jj = lane2 % c
    eye2 = (ii == jj).astype(F32)
    bdmask = _iota((LANES, LANES), 0) // c == _iota((LANES, LANES), 1) // c
    ri = _iota((c, c), 0)
    ci = _iota((c, c), 1)
    tri_f = (ri >= ci).astype(F32).astype(BF16)
    tri_b = (ri <= ci).astype(F32).astype(BF16)
    first_half = lane2 < DN_DK
    nh = DN_HEADS

    def chunk_group(grp, carry):
        systems = []
        for bb, cc in [(bb, cc) for bb in range(nb) for cc in range(DN_CHUNKS_PER_GROUP)]:
            ch = grp * DN_CHUNKS_PER_GROUP + cc
            rows = pl.ds(pl.multiple_of(ch * c, c), c)
            el_rows = pl.ds(pl.multiple_of(ch * SUBLANES, SUBLANES), SUBLANES)
            gc = g_s[bb, rows, :]
            bc = beta_s[bb, rows, :]
            gam = jnp.where(lane2 < nh, _sel_dot(tri_f, gc), _sel_dot(tri_b, gc))
            gam_t = jnp.concatenate([gam, gam], axis=0).T
            for d in range(2):
                incl = (ii >= jj) if d == 0 else (ii <= jj)
                strict = (ii > jj) if d == 0 else (ii < jj)
                for p in range(nh // 2):
                    l0 = d * nh + 2 * p
                    l1 = l0 + 1
                    gcol = jnp.where(first_half, gam[:, l0:l0 + 1], gam[:, l1:l1 + 1])
                    grow = jnp.where(first_half, gam_t[l0:l0 + 1, :], gam_t[l1:l1 + 1, :])
                    bcol = jnp.where(first_half, bc[:, 2 * nh + l0:2 * nh + l0 + 1],
                                     bc[:, 2 * nh + l1:2 * nh + l1 + 1])
                    kg = qkv_s[bb, rows, BR_WIDTH + p * LANES:BR_WIDTH + (p + 1) * LANES]
                    systems.append(dict(
                        bb=bb, d=d, rows=rows, el_rows=el_rows, lanes=slice(p * LANES, (p + 1) * LANES),
                        strict=strict, gcol=gcol, bcol=bcol, kg=kg, kb=kg * bcol,
                        dec=jnp.where(incl, jnp.exp(jnp.where(incl, gcol - grow, 0.0)), 0.0),
                        qg=qkv_s[bb, rows, p * LANES:(p + 1) * LANES],
                        vg=qkv_s[bb, rows, 2 * BR_WIDTH + p * LANES:2 * BR_WIDTH + (p + 1) * LANES],
                        kstack=_pair_stack(kg, lane2).astype(BF16)))
        for sy in systems:
            both = _dot_nt(jnp.concatenate([sy["kb"], sy["qg"]], axis=0).astype(BF16), sy["kstack"])
            sy["pm"] = -jnp.where(sy["strict"], both[0:c] * sy["dec"], 0.0)
            qk_ref[sy["bb"], sy["d"], sy["rows"], sy["lanes"]] = (both[c:2 * c] * sy["dec"]).astype(BF16)
        for sy in systems:
            sy["t"] = eye2 + sy["pm"]
            sy["pm"] = _dot(sy["pm"].astype(BF16), _pair_blockdiag(sy["pm"], bdmask).astype(BF16))
        for it in range(5):
            for sy in systems:
                sy["bd"] = _pair_blockdiag(sy["pm"], bdmask).astype(BF16)
            for sy in systems:
                if it < 4:
                    both = _dot(jnp.concatenate([sy["t"], sy["pm"]], axis=0).astype(BF16), sy["bd"])
                    sy["t"] = sy["t"] + both[0:c]
                    sy["pm"] = both[c:2 * c]
                else:
                    sy["t"] = sy["t"] + _dot(sy["t"].astype(BF16), sy["bd"])
        for sy in systems:
            sy["eg"] = jnp.exp(sy["gcol"])
            rhs = jnp.concatenate([_pair_stack(sy["vg"] * sy["bcol"], lane2),
                                   _pair_stack(sy["kb"] * sy["eg"], lane2)], axis=1).astype(BF16)
            uw = _dot(sy["t"].astype(BF16), rhs)
            u_ref[sy["bb"], sy["d"], sy["rows"], sy["lanes"]] = uw[:, 0:LANES]
            w_ref[sy["bb"], sy["d"], sy["rows"], sy["lanes"]] = uw[:, LANES:2 * LANES].astype(BF16)
        for sy in systems:
            gcol = sy["gcol"]
            glast = gcol[c - 1:c, :] if sy["d"] == 0 else gcol[0:1, :]
            qd_ref[sy["bb"], sy["d"], sy["rows"], sy["lanes"]] = (sy["qg"] * sy["eg"]).astype(BF16)
            kd_ref[sy["bb"], sy["d"], sy["rows"], sy["lanes"]] = (sy["kg"] * jnp.exp(glast - gcol)).astype(BF16)
            el_ref[sy["bb"], sy["d"], sy["el_rows"], sy["lanes"]] = jnp.broadcast_to(jnp.exp(glast), (SUBLANES, LANES))
        return carry

    lax.fori_loop(0, tm // (c * DN_CHUNKS_PER_GROUP), chunk_group, 0)


def _dn_chunk(pd, pab, conv_w, alog, dtb, n_ctx):
    b, s, _ = pd.shape
    tm = TOKEN_TILE
    nt = s // tm
    nsub = tm // SUBLANES
    last_blk = s // SUBLANES - 1
    full = lambda a: pl.BlockSpec(a.shape, lambda i, j: (0,) * a.ndim)
    nb = _dn_batch_per_step(b)
    dir_tok = lambda: pl.BlockSpec((nb, 2, tm, BR_WIDTH), lambda i, j: (i, 0, j, 0))
    dir_shape = lambda dt: jax.ShapeDtypeStruct((b, 2, s, BR_WIDTH), dt)
    el_rows = (tm // DN_CHUNK) * SUBLANES
    return pl.pallas_call(
        functools.partial(_dn_chunk_body, n_ctx_tiles=n_ctx // tm, n_tiles=nt),
        grid=(b // nb, nt),
        in_specs=[pl.BlockSpec((nb, tm, PD_COLS), lambda i, j: (i, j, 0)),
                  pl.BlockSpec((nb, SUBLANES, PD_COLS), lambda i, j: (i, jnp.maximum(j * nsub - 1, 0), 0)),
                  pl.BlockSpec((nb, SUBLANES, PD_COLS),
                               lambda i, j: (i, jnp.minimum((j + 1) * nsub, last_blk), 0)),
                  pl.BlockSpec((nb, tm, PAB_COLS), lambda i, j: (i, j, 0)),
                  full(conv_w), full(alog), full(dtb)],
        out_specs=[dir_tok(), dir_tok(), dir_tok(), dir_tok(), dir_tok(),
                   pl.BlockSpec((nb, 2, el_rows, BR_WIDTH), lambda i, j: (i, 0, j, 0))],
        out_shape=[dir_shape(F32)] + [dir_shape(BF16)] * 4
        + [jax.ShapeDtypeStruct((b, 2, nt * el_rows, BR_WIDTH), F32)],
        scratch_shapes=[pltpu.VMEM((nb, tm, PD_COLS), F32), pltpu.VMEM((nb, tm, PAB_COLS), F32),
                        pltpu.VMEM((nb, tm, PAB_COLS), F32)],
        compiler_params=_cparams(("arbitrary", "arbitrary")),
        name="dn_chunk",
    )(pd, pd, pd, pab, conv_w, alog, dtb)


def _dn_scan_body(uf, wf, qkf, qdf, kdf, elf, ub, wb, qkb, qdb, kdb, elb, of_ref, ob_ref, s_ref):
    nb, tm = uf.shape[0], uf.shape[2]
    c = DN_CHUNK
    n_chunks = tm // c

    @pl.when(pl.program_id(1) == 0)
    def _():
        s_ref[...] = jnp.zeros_like(s_ref)

    bdmask = _iota((LANES, LANES), 0) // c == _iota((LANES, LANES), 1) // c
    views = ((uf, wf, qkf, qdf, kdf, elf, of_ref), (ub, wb, qkb, qdb, kdb, elb, ob_ref))
    chains = [(bb, d, p) for bb in range(nb) for d in range(2) for p in range(DN_HEADS // 2)]
    state = {ch: s_ref[ch] for ch in chains}
    for step in range(n_chunks):
        blk = {}
        for bb, d, p in chains:
            ck = step if d == 0 else n_chunks - 1 - step
            blk[bb, d, p] = (slice(ck * c, (ck + 1) * c), slice(p * LANES, (p + 1) * LANES), ck)
        tile = lambda k, ch: views[ch[1]][k][ch[0], 0, blk[ch][0], blk[ch][1]]
        sb = {ch: state[ch].astype(BF16) for ch in chains}
        ws = {ch: _dot(tile(1, ch), sb[ch]) for ch in chains}
        qs = {ch: _dot(tile(3, ch), sb[ch]) for ch in chains}
        vb = {ch: (tile(0, ch) - ws[ch]).astype(BF16) for ch in chains}
        for ch in chains:
            vbd = jnp.where(bdmask, jnp.concatenate([vb[ch], vb[ch]], axis=0), jnp.zeros((), BF16))
            views[ch[1]][6][ch[0], blk[ch][0], blk[ch][1]] = qs[ch] + _dot(tile(2, ch), vbd)
        upd = {ch: _dot_tn(tile(4, ch), vb[ch]) for ch in chains}
        for ch in chains:
            ck = blk[ch][2]
            el = views[ch[1]][5][ch[0], 0, ck * SUBLANES:ck * SUBLANES + 1, blk[ch][1]]
            state[ch] = state[ch] * el + jnp.where(bdmask, upd[ch], 0.0)
    for ch in chains:
        s_ref[ch] = state[ch]


def _dn_scan(u, w, qk, qd, kd, el, n_ctx):
    b, _, s, _ = u.shape
    tm = TOKEN_TILE
    nb = _dn_batch_per_step(b)
    nt = s // tm
    nctx = n_ctx // tm
    el_rows = (tm // DN_CHUNK) * SUBLANES

    def rev(j):
        return jnp.where(j < nctx, nctx - 1 - j, nt - 1 - (j - nctx))

    fwd = lambda rows: pl.BlockSpec((nb, 1, rows, BR_WIDTH), lambda i, j: (i, 0, j, 0))
    bwd = lambda rows: pl.BlockSpec((nb, 1, rows, BR_WIDTH), lambda i, j: (i, 1, rev(j), 0))
    in_specs = [fwd(tm)] * 5 + [fwd(el_rows)] + [bwd(tm)] * 5 + [bwd(el_rows)]
    out_shape = jax.ShapeDtypeStruct((b, s, BR_WIDTH), F32)
    return pl.pallas_call(
        _dn_scan_body,
        grid=(b // nb, nt),
        in_specs=in_specs,
        out_specs=[pl.BlockSpec((nb, tm, BR_WIDTH), lambda i, j: (i, j, 0)),
                   pl.BlockSpec((nb, tm, BR_WIDTH), lambda i, j: (i, rev(j), 0))],
        out_shape=[out_shape, out_shape],
        scratch_shapes=[pltpu.VMEM((nb, 2, DN_HEADS // 2, LANES, LANES), F32)],
        compiler_params=_cparams(("arbitrary", "arbitrary")),
        name="dn_scan",
    )(u, w, qk, qd, kd, el, u, w, qk, qd, kd, el)


def _merge_body(x_ref, mod_ref, h_ref, pg_ref, ya_ref, yb_ref, yc_ref, of_ref, ob_ref, dng_ref,
                wg_ref, wb_ref, wo_ref, lng_ref, lnb_ref, o_ref, *, alpha):
    hb = h_ref[0]
    o = of_ref[0] + ob_ref[0]
    ms = _sum_dot(o * o, _block_ones(BR_WIDTH, DN_DV).astype(BF16), 2) * (1.0 / DN_DV)
    yd = o * lax.rsqrt(ms + LN_EPS) * dng_ref[...]
    pg = pg_ref[0]
    sg = pg * _sigmoid(pg)
    acc = None
    for i, y in enumerate((ya_ref[0], yb_ref[0], yc_ref[0], yd)):
        t = (y * sg[:, i * BR_WIDTH:(i + 1) * BR_WIDTH]).astype(BF16)
        term = _sigmoid(_dot(hb, wg_ref[i])) * _dot(t, wb_ref[i])
        acc = term if acc is None else acc + term
    out = _dot(acc.astype(BF16), wo_ref[...])
    gt = mod_ref[0, 0, 2:3, :]
    r = alpha * x_ref[0] + gt * out
    mu = jnp.mean(r, -1, keepdims=True)
    rc = r - mu
    var = jnp.mean(rc * rc, -1, keepdims=True)
    o_ref[0] = rc * lax.rsqrt(var + LN_EPS) * lng_ref[...] + lnb_ref[...]


def _merge(xs, mod, h, pg, ya, yb, yc, o_f, o_b, dng, wg, wb, wo, lng, lnb, n_ctx_tiles, skip_tiles, alpha):
    b, s, d = xs.shape
    tm = TOKEN_TILE
    nt = s // tm - skip_tiles
    tok = lambda width: pl.BlockSpec((1, tm, width), lambda i, j: (i, j + skip_tiles, 0))
    full = lambda a: pl.BlockSpec(a.shape, lambda i, j: (0,) * a.ndim)
    return pl.pallas_call(
        functools.partial(_merge_body, alpha=alpha),
        grid=(b, nt),
        in_specs=[tok(d),
                  pl.BlockSpec((1, 1, 3, d),
                               lambda i, j: (i, jnp.where(j + skip_tiles >= n_ctx_tiles, 1, 0), 0, 0)),
                  tok(d), tok(PG_COLS), tok(BR_WIDTH), tok(BR_WIDTH), tok(BR_WIDTH), tok(BR_WIDTH),
                  tok(BR_WIDTH), full(dng), full(wg), full(wb), full(wo), full(lng), full(lnb)],
        out_specs=pl.BlockSpec((1, tm, d), lambda i, j: (i, j, 0)),
        out_shape=jax.ShapeDtypeStruct((b, nt * tm, d), F32),
        compiler_params=_cparams(("arbitrary", "arbitrary")),
        name="merge",
    )(xs, mod, h, pg, ya, yb, yc, o_f, o_b, dng, wg, wb, wo, lng, lnb)


def _rope_swap_index(n_blocks):
    blk = jnp.array(list(range(8, 16)) + list(range(0, 8)) + list(range(24, 32)) + list(range(16, 24)))
    return (jnp.arange(n_blocks)[:, None] * 32 + blk[None, :]).reshape(-1)


def _pack_w_in(w):
    d = w.shape[0]
    o = 0
    cq = w[:, o:o + MLA_Q_LORA]; o += MLA_Q_LORA
    ckv = w[:, o:o + MLA_KV_LORA]; o += MLA_KV_LORA
    kr = w[:, o:o + MLA_ROPE]; o += MLA_ROPE
    pb = w[:, o:o + GMLP_COLS]; o += GMLP_COLS
    dq = w[:, o:o + 256]; dk = w[:, o + 256:o + 512]; dv = w[:, o + 512:o + 768]; o += DIFF_COLS
    dn_qkv = w[:, o:o + 3 * BR_WIDTH]; dn_ab = w[:, o + 3 * BR_WIDTH:o + DN_COLS]; o += DN_COLS
    pg = w[:, o:]
    z = lambda n: jnp.zeros((d, n), w.dtype)
    sw1 = _rope_swap_index(1)
    sw8 = _rope_swap_index(8)
    place = lambda a: jnp.concatenate([z(MLA_NOPE), a, z(LANES - MLA_NOPE - MLA_ROPE)], axis=1)
    packed = jnp.concatenate(
        [cq, ckv, place(kr), place(kr[:, sw1]), pb, dq, dk, dv, dq[:, sw8], dk[:, sw8], dn_qkv,
         dn_ab, z(PAB_COLS - 4 * DN_HEADS), pg], axis=1)
    return packed.astype(BF16)


def _pack_mla_weights(w_uq, w_ukv):
    dq = MLA_NOPE + MLA_ROPE
    wq = w_uq.reshape(MLA_Q_LORA, MLA_HEADS, dq)
    zq = jnp.zeros((MLA_Q_LORA, MLA_HEADS, LANES - dq), w_uq.dtype)
    wq_p = jnp.concatenate([wq, zq], axis=2).reshape(MLA_Q_LORA, MLA_HEADS * LANES)
    rope_sw = wq[:, :, MLA_NOPE:][:, :, _rope_swap_index(1)]
    wqs_p = jnp.concatenate([jnp.zeros((MLA_Q_LORA, MLA_HEADS, MLA_NOPE), w_uq.dtype), rope_sw, zq],
                            axis=2).reshape(MLA_Q_LORA, MLA_HEADS * LANES)
    wkv = w_ukv.reshape(MLA_KV_LORA, MLA_HEADS, MLA_NOPE + MLA_V)
    zk = jnp.zeros((MLA_KV_LORA, MLA_HEADS, LANES - MLA_NOPE), w_ukv.dtype)
    wk_p = jnp.concatenate([wkv[:, :, :MLA_NOPE], zk], axis=2).reshape(MLA_KV_LORA, MLA_HEADS * LANES)
    wv_p = wkv[:, :, MLA_NOPE:].reshape(MLA_KV_LORA, MLA_HEADS * MLA_V)
    return wq_p.astype(BF16), wqs_p.astype(BF16), wk_p.astype(BF16), wv_p.astype(BF16)


def _rope_tables(n, n_ctx):
    rows = n // GRID_W
    row = jnp.repeat(jnp.arange(rows, dtype=F32), GRID_W)
    col = jnp.tile(jnp.arange(GRID_W, dtype=F32), rows)
    axis_dim = MLA_ROPE // 2
    inv_freq = ROPE_BASE ** (-jnp.arange(0, axis_dim, 2, dtype=F32) / axis_dim)
    ar = row[:, None] * inv_freq
    ac = col[:, None] * inv_freq
    cos32 = jnp.concatenate([jnp.cos(ar), jnp.cos(ar), jnp.cos(ac), jnp.cos(ac)], axis=1)
    sin32 = jnp.concatenate([-jnp.sin(ar), jnp.sin(ar), -jnp.sin(ac), jnp.sin(ac)], axis=1)
    cos32 = jnp.concatenate([jnp.ones((n_ctx, 32), F32), cos32], axis=0)
    sin32 = jnp.concatenate([jnp.zeros((n_ctx, 32), F32), sin32], axis=0)
    s = n + n_ctx
    one = lambda w: jnp.ones((s, w), F32)
    zero = lambda w: jnp.zeros((s, w), F32)
    mla_cos = jnp.concatenate([one(MLA_NOPE), cos32, one(LANES - MLA_NOPE - MLA_ROPE)], axis=1)
    mla_sin = jnp.concatenate([zero(MLA_NOPE), sin32, zero(LANES - MLA_NOPE - MLA_ROPE)], axis=1)
    diff_cos = jnp.tile(cos32, (1, LANES // 32))
    diff_sin = jnp.tile(sin32, (1, LANES // 32))
    return mla_cos, mla_sin, diff_cos, diff_sin


def _pad_lanes(a, width=LANES):
    return jnp.concatenate([a, jnp.zeros(a.shape[:-1] + (width - a.shape[-1],), a.dtype)], axis=-1)


def kernel(x, c, ctx, c_ctx, w_mod, b_mod, w_in, mla_q_norm, mla_w_uq, mla_kv_norm, mla_w_ukv, gmlp_ln_g, gmlp_w_s, gmlp_b_s, diff_lq1, diff_lk1, diff_lq2, diff_lk2, diff_norm_g, dn_conv_w, dn_a_log, dn_dt_bias, dn_norm_g, w_gate, w_branch, w_out, ln_g, ln_b):
    b, n, d = x.shape
    n_ctx = ctx.shape[1]
    depth = w_mod.shape[0]
    tm = TOKEN_TILE
    assert d == D_MODEL and n % tm == 0 and n_ctx % tm == 0 and n % GRID_W == 0
    assert tm == ATTN_Q_TILE == ATTN_K_TILE
    alpha = (2 * depth) ** 0.25
    n_ctx_tiles = n_ctx // tm

    xs = jnp.concatenate([ctx, x], axis=1)
    mla_cos, mla_sin, diff_cos, diff_sin = _rope_tables(n, n_ctx)

    rows = ((b + 1 + SUBLANES - 1) // SUBLANES) * SUBLANES
    cc = jnp.concatenate([c, c_ctx[None, :], jnp.zeros((rows - b - 1, d), F32)], axis=0)
    mod_all = _modulation(cc, w_mod, b_mod)

    for l in range(depth):
        last = l == depth - 1
        lam_init = 0.8 - 0.6 * math.exp(-0.3 * l)
        mod_l = mod_all[l].reshape(rows, 3, d)
        mod = jnp.stack([jnp.broadcast_to(mod_l[b][None], (b, 3, d)), mod_l[:b]], axis=1)

        wq, wqs, wk, wv = _pack_mla_weights(mla_w_uq[l], mla_w_ukv[l])
        bias = jnp.repeat(gmlp_b_s[l].T, BR_WIDTH // GMLP_GROUPS, axis=1)
        h, pd, pab, pg, q_a, k_a, vt_a, yb, q_c, k_c, vt_c = _inproj(
            xs, mod, _pack_w_in(w_in[l]), (mla_cos, mla_sin, diff_cos, diff_sin),
            (mla_q_norm[l][None, :], mla_kv_norm[l][None, :], wq, wqs, wk, wv),
            (gmlp_ln_g[l][None, :], gmlp_w_s[l].astype(BF16), bias), n_ctx_tiles)
        ya = _mla_attn(q_a, k_a, vt_a, n_ctx)
        lqk = _pad_lanes(jnp.stack([diff_lq1[l], diff_lk1[l], diff_lq2[l], diff_lk2[l]], axis=0))
        yc = _diff_attn(q_c, k_c, vt_c, lqk, diff_norm_g[l][:, None], n_ctx, lam_init)

        conv_w = jnp.concatenate([dn_conv_w[l], jnp.zeros((SUBLANES - 3, PD_COLS), F32)], axis=0)
        alog = _pad_lanes(dn_a_log[l].reshape(1, 2 * DN_HEADS))
        dtb = _pad_lanes(dn_dt_bias[l].reshape(1, 2 * DN_HEADS))
        u, w, qk, qd, kd, el = _dn_chunk(pd, pab, conv_w, alog, dtb, n_ctx)
        o_f, o_b = _dn_scan(u, w, qk, qd, kd, el, n_ctx)

        dng = jnp.tile(dn_norm_g[l], DN_HEADS)[None, :]
        xs = _merge(xs, mod, h, pg, ya, yb, yc, o_f, o_b, dng, w_gate[l].astype(BF16),
                    w_branch[l].astype(BF16), w_out[l].astype(BF16), ln_g[l][None, :], ln_b[l][None, :],
                    n_ctx_tiles, n_ctx_tiles if last else 0, alpha)
    return xs
```

```python
import functools
import math

import jax
import jax.numpy as jnp
from jax import lax
from jax.experimental import pallas as pl
from jax.experimental.pallas import tpu as pltpu

F32 = jnp.float32
BF16 = jnp.bfloat16

D_MODEL = 1024
GRID_W = 64
N_BRANCH = 4
BR_WIDTH = 256
MLA_HEADS = 4
MLA_NOPE = 64
MLA_ROPE = 32
MLA_V = 64
MLA_Q_LORA = 256
MLA_KV_LORA = 128
GMLP_GROUPS = 4
GMLP_CHUNK = 128
DIFF_HEADS = 4
DIFF_D = 32
DN_HEADS = 4
DN_DK = 64
DN_DV = 64
DN_CHUNK = 64
ROPE_BASE = 10000.0
LN_EPS = 1e-6

MLA_COLS = MLA_Q_LORA + MLA_KV_LORA + MLA_ROPE
GMLP_COLS = 2 * BR_WIDTH
DIFF_COLS = 3 * DIFF_HEADS * 2 * DIFF_D
DN_COLS = 3 * BR_WIDTH + 4 * DN_HEADS

LANES = 128
SUBLANES = 8
VMEM_LIMIT_BYTES = 56 * 1024 * 1024

TOKEN_TILE = 256
ATTN_Q_TILE = 256
ATTN_K_TILE = 256
DN_CHUNKS_PER_GROUP = 4
DN_BATCH_PER_STEP = 2
DN_INV_BLOCK = 8
ATTN_STEPS_PER_TRIP = 128
ATTN_LOOKAHEAD = 5
NEG_BIG = -1e30
VT_ROWS = 80
LOG2_E = 1.4426950408889634

PA_COLS = 640
PB_COLS = 512
PC_COLS = 1280
PD_COLS = 768
PAB_COLS = 128
PG_COLS = 1024
PACK_SPLITS = (PA_COLS, PB_COLS, PC_COLS, PD_COLS, PAB_COLS, PG_COLS)
PACK_COLS = sum(PACK_SPLITS)


def _cparams(semantics):
    return pltpu.CompilerParams(dimension_semantics=semantics, vmem_limit_bytes=VMEM_LIMIT_BYTES)


def _dot(a, b):
    return jnp.dot(a, b, preferred_element_type=F32)


def _dot_nt(a, b):
    return lax.dot_general(a, b, (((1,), (1,)), ((), ())), preferred_element_type=F32)


def _dot_tn(a, b):
    return lax.dot_general(a, b, (((0,), (0,)), ((), ())), preferred_element_type=F32)


def _split_bf16(x, terms):
    pieces = []
    for _ in range(terms):
        hi = x.astype(BF16)
        pieces.append(hi)
        x = x - hi.astype(F32)
    return pieces


def _sum_dot(x, sel, terms=3):
    return sum(_dot(p, sel) for p in _split_bf16(x, terms))


def _sel_dot(sel, x, terms=3):
    return sum(_dot(sel, p) for p in _split_bf16(x, terms))


def _sigmoid(x):
    return 1.0 / (1.0 + jnp.exp(-x))


def _iota(shape, axis):
    return lax.broadcasted_iota(jnp.int32, shape, axis)


def _block_ones(n, blk):
    return (_iota((n, n), 0) // blk == _iota((n, n), 1) // blk).astype(F32)


def _mod_body(c_ref, w_ref, b_ref, o_ref):
    c = c_ref[...]
    s = (c * _sigmoid(c)).astype(BF16)
    o_ref[0] = _dot(s, w_ref[0].astype(BF16)) + b_ref[0]


def _modulation(cc, w_mod, b_mod):
    depth, d, n3 = w_mod.shape
    rows = cc.shape[0]
    tn = 1024
    return pl.pallas_call(
        _mod_body,
        grid=(depth, n3 // tn),
        in_specs=[pl.BlockSpec((rows, d), lambda l, j: (0, 0)),
                  pl.BlockSpec((1, d, tn), lambda l, j: (l, 0, j)),
                  pl.BlockSpec((1, 1, tn), lambda l, j: (l, 0, j))],
        out_specs=pl.BlockSpec((1, rows, tn), lambda l, j: (l, 0, j)),
        out_shape=jax.ShapeDtypeStruct((depth, rows, n3), F32),
        compiler_params=_cparams(("arbitrary", "arbitrary")),
        name="modulation",
    )(cc, w_mod, b_mod.reshape(depth, 1, n3))


def _rms(x, g):
    return x * lax.rsqrt(jnp.mean(x * x, -1, keepdims=True) + LN_EPS) * g


def _mla_prep_values(pa, cos, sin, qn, kvn, wq, wqs, wk, wv, scale):
    cq = pa[:, 0:MLA_Q_LORA]
    ckv = pa[:, MLA_Q_LORA:MLA_Q_LORA + MLA_KV_LORA]
    kr = pa[:, 384:512]
    kr_sw = pa[:, 512:640]
    cos4 = jnp.concatenate([cos] * MLA_HEADS, axis=1)
    sin4 = jnp.concatenate([sin] * MLA_HEADS, axis=1)
    cqn = _rms(cq, qn).astype(BF16)
    q = ((_dot(cqn, wq) * cos4 + _dot(cqn, wqs) * sin4) * scale).astype(BF16)
    ckvn = _rms(ckv, kvn).astype(BF16)
    kr_rot = kr * cos + kr_sw * sin
    k = (_dot(ckvn, wk) + jnp.concatenate([kr_rot] * MLA_HEADS, axis=1)).astype(BF16)
    return q, k, _vt_with_ones(_dot(ckvn, wv), MLA_HEADS, MLA_V)


def _diff_prep_values(pc, cos, sin, scale):
    cos = jnp.concatenate([cos] * 2, axis=1)
    sin = jnp.concatenate([sin] * 2, axis=1)
    q = ((pc[:, 0:256] * cos + pc[:, 768:1024] * sin) * scale).astype(BF16)
    k = (pc[:, 256:512] * cos + pc[:, 1024:1280] * sin).astype(BF16)
    return q, k, _vt_with_ones(pc[:, 512:768], DIFF_HEADS, 2 * DIFF_D)


def _gelu_tanh(x):
    return 0.5 * x * (1.0 + jnp.tanh(math.sqrt(2.0 / math.pi) * (x + 0.044715 * (x * x * x))))


def _gmlp_values(pb, g, ws_ref, bias_ref, o_ref):
    z = _gelu_tanh(pb)
    u = z[:, 0:BR_WIDTH]
    v = z[:, BR_WIDTH:2 * BR_WIDTH]
    mu = jnp.mean(v, -1, keepdims=True)
    vc = v - mu
    var = jnp.mean(vc * vc, -1, keepdims=True)
    v = (vc * lax.rsqrt(var + LN_EPS) * g).astype(BF16)
    lane = _iota((GMLP_CHUNK, LANES), 1)
    gw = BR_WIDTH // GMLP_GROUPS
    for c in range(pb.shape[0] // GMLP_CHUNK):
        rows = slice(c * GMLP_CHUNK, (c + 1) * GMLP_CHUNK)
        for half in range(BR_WIDTH // LANES):
            cols = slice(half * LANES, (half + 1) * LANES)
            vch = v[rows, cols]
            a0 = _dot(ws_ref[2 * half], vch)
            a1 = _dot(ws_ref[2 * half + 1], vch)
            mixed = jnp.where(lane < gw, a0, a1) + bias_ref[:, cols]
            o_ref[0, rows, cols] = u[rows, cols] * mixed


def _inproj_body(x_ref, mod_ref, w_ref, cosm_ref, sinm_ref, cosd_ref, sind_ref, qn_ref, kvn_ref,
                 wq_ref, wqs_ref, wk_ref, wv_ref, gg_ref, ws_ref, bias_ref,
                 h_ref, pd_ref, pab_ref, pg_ref, qa_ref, ka_ref, vta_ref, yb_ref, qc_ref, kc_ref, vtc_ref,
                 *, mla_scale, diff_scale):
    x = x_ref[0]
    mu = jnp.mean(x, -1, keepdims=True)
    xc = x - mu
    var = jnp.mean(xc * xc, -1, keepdims=True)
    xn = xc * lax.rsqrt(var + LN_EPS)
    sh = mod_ref[0, 0, 0:1, :]
    sc = mod_ref[0, 0, 1:2, :]
    hb = (xn * (1.0 + sc) + sh).astype(BF16)
    h_ref[0] = hb
    offs = [sum(PACK_SPLITS[:i]) for i in range(len(PACK_SPLITS))]
    proj = lambda i: _dot(hb, w_ref[:, offs[i]:offs[i] + PACK_SPLITS[i]])
    pa = proj(0)
    pb = proj(1)
    pc = proj(2)
    qa_ref[0], ka_ref[0], vta_ref[0, 0] = _mla_prep_values(
        pa, cosm_ref[...], sinm_ref[...], qn_ref[...], kvn_ref[...], wq_ref[...], wqs_ref[...],
        wk_ref[...], wv_ref[...], mla_scale)
    pd_ref[0] = proj(3)
    _gmlp_values(pb, gg_ref[...], ws_ref, bias_ref, yb_ref)
    pg_ref[0] = proj(5)
    qc_ref[0], kc_ref[0], vtc_ref[0, 0] = _diff_prep_values(pc, cosd_ref[...], sind_ref[...], diff_scale)
    pab_ref[0] = proj(4)


def _inproj(xs, mod, w_pack, tables, mla_params, gmlp_params, n_ctx_tiles):
    b, s, d = xs.shape
    tm = TOKEN_TILE
    hp = MLA_HEADS * LANES
    tok = lambda width, dt: jax.ShapeDtypeStruct((b, s, width), dt)
    tok_spec = lambda width: pl.BlockSpec((1, tm, width), lambda i, j: (i, j, 0))
    full = lambda a: pl.BlockSpec(a.shape, lambda i, j: (0,) * a.ndim)
    table = pl.BlockSpec((tm, LANES), lambda i, j: (j, 0))
    vt_spec = pl.BlockSpec((1, 1, 4 * VT_ROWS, tm), lambda i, j: (i, j, 0, 0))
    vt_shape = jax.ShapeDtypeStruct((b, s // tm, 4 * VT_ROWS, tm), BF16)
    consts = list(mla_params) + list(gmlp_params)
    return pl.pallas_call(
        functools.partial(_inproj_body, mla_scale=(MLA_NOPE + MLA_ROPE) ** -0.5 * LOG2_E,
                          diff_scale=DIFF_D ** -0.5 * LOG2_E),
        grid=(b, s // tm),
        in_specs=[tok_spec(d),
                  pl.BlockSpec((1, 1, 3, d), lambda i, j: (i, jnp.where(j >= n_ctx_tiles, 1, 0), 0, 0)),
                  pl.BlockSpec((d, PACK_COLS), lambda i, j: (0, 0))]
        + [table] * 4 + [full(a) for a in consts],
        out_specs=[tok_spec(d), tok_spec(PD_COLS), tok_spec(PAB_COLS), tok_spec(PG_COLS),
                   tok_spec(hp), tok_spec(hp), vt_spec, tok_spec(BR_WIDTH),
                   tok_spec(BR_WIDTH), tok_spec(BR_WIDTH), vt_spec],
        out_shape=[tok(d, BF16), tok(PD_COLS, F32), tok(PAB_COLS, F32), tok(PG_COLS, F32),
                   tok(hp, BF16), tok(hp, BF16), vt_shape, tok(BR_WIDTH, F32),
                   tok(BR_WIDTH, BF16), tok(BR_WIDTH, BF16), vt_shape],
        compiler_params=_cparams(("arbitrary", "arbitrary")),
        name="inproj",
    )(xs, mod, w_pack, *tables, *consts)


def _vt_with_ones(v, heads, dv):
    tm = v.shape[0]
    vt = v.T
    aug = (_iota((VT_ROWS - dv, tm), 0) == 0).astype(F32)
    pieces = []
    for h in range(heads):
        pieces += [vt[h * dv:(h + 1) * dv], aug]
    return jnp.concatenate(pieces, axis=0).astype(BF16)


def _tiles_per_iter(n_tiles, n_chain):
    t = max(1, ATTN_STEPS_PER_TRIP // n_chain)
    while n_tiles % t:
        t -= 1
    return t


def _flash_chains(qt_s, k_ref, vt_ref, k_lanes, v_rows, n_k, n_first, tiles_per_iter, m_s, acc_s):
    tk = ATTN_K_TILE
    m_s[...] = jnp.full(m_s.shape, NEG_BIG, F32)
    acc_s[...] = jnp.zeros(acc_s.shape, F32)

    n_chain = len(k_lanes)

    def scores(c, kt):
        k_rows = pl.ds(pl.multiple_of(kt * tk, tk), tk)
        return _dot(k_ref[0, k_rows, k_lanes[c][0]:k_lanes[c][1]], qt_s[c])

    def run(kt0, n_iter, tiles_per_iter):
        per_iter = tiles_per_iter * n_chain

        def step(it, ahead):
            ahead = list(ahead)
            base = kt0 + it * tiles_per_iter
            for idx in range(per_iter):
                kt, c = base + idx // n_chain, idx % n_chain
                s = ahead.pop(0)
                nxt = idx + ATTN_LOOKAHEAD
                ahead.append(scores(nxt % n_chain, jnp.minimum(base + nxt // n_chain, n_k - 1)))
                m_old = m_s[c]
                m_new = jnp.maximum(m_old, jnp.max(s, axis=0, keepdims=True))
                p = jnp.exp2(s - m_new).astype(BF16)
                vt = vt_ref[0, kt, v_rows[c][0]:v_rows[c][1], :]
                acc_s[c] = jnp.exp2(m_old - m_new) * acc_s[c] + _dot(vt, p)
                m_s[c] = m_new
            return tuple(ahead)

        init = tuple(scores(i % n_chain, jnp.minimum(kt0 + i // n_chain, n_k - 1))
                     for i in range(ATTN_LOOKAHEAD))
        lax.fori_loop(0, n_iter, step, init)

    run(0, n_first, 1)
    run(n_first, (n_k - n_first) // tiles_per_iter, tiles_per_iter)


def _mla_attn_body(q_ref, k_ref, vt_ref, o_ref, qt_s, m_s, acc_s, *, n_ctx_q_tiles, n_ctx_k_tiles,
                   n_k_tiles):
    j = pl.program_id(1)
    n_k = jnp.where(j < n_ctx_q_tiles, n_ctx_k_tiles, n_k_tiles)
    k_lanes = [(h * LANES, (h + 1) * LANES) for h in range(MLA_HEADS)]
    v_rows = [(h * VT_ROWS, (h + 1) * VT_ROWS) for h in range(MLA_HEADS)]
    for h in range(MLA_HEADS):
        qt_s[h] = q_ref[0, :, h * LANES:(h + 1) * LANES].astype(F32).T.astype(BF16)
    _flash_chains(qt_s, k_ref, vt_ref, k_lanes, v_rows, n_k, n_ctx_k_tiles,
                  _tiles_per_iter(n_k_tiles - n_ctx_k_tiles, MLA_HEADS), m_s, acc_s)
    outs = []
    for h in range(MLA_HEADS):
        acc = acc_s[h]
        outs.append(acc[0:MLA_V] / acc[MLA_V:MLA_V + 1])
    o_ref[0] = jnp.concatenate(outs, axis=0).T


def _mla_attn(q, k, vt, n_ctx):
    b, s, hp = q.shape
    tq, tk = ATTN_Q_TILE, ATTN_K_TILE
    return pl.pallas_call(
        functools.partial(_mla_attn_body, n_ctx_q_tiles=n_ctx // tq, n_ctx_k_tiles=n_ctx // tk,
                          n_k_tiles=s // tk),
        grid=(b, s // tq),
        in_specs=[pl.BlockSpec((1, tq, hp), lambda i, j: (i, j, 0)),
                  pl.BlockSpec((1, s, hp), lambda i, j: (i, 0, 0)),
                  pl.BlockSpec((1, s // tk, MLA_HEADS * VT_ROWS, tk), lambda i, j: (i, 0, 0, 0))],
        out_specs=pl.BlockSpec((1, tq, BR_WIDTH), lambda i, j: (i, j, 0)),
        out_shape=jax.ShapeDtypeStruct((b, s, BR_WIDTH), F32),
        scratch_shapes=[pltpu.VMEM((MLA_HEADS, LANES, tq), BF16), pltpu.VMEM((MLA_HEADS, 1, tq), F32),
                        pltpu.VMEM((MLA_HEADS, VT_ROWS, tq), F32)],
        compiler_params=_cparams(("arbitrary", "arbitrary")),
        name="mla_attn",
    )(q, k, vt)


def _diff_attn_body(q_ref, k_ref, vt_ref, lqk_ref, g_ref, o_ref, qm_s, m_s, acc_s, *, n_ctx_q_tiles,
                    n_ctx_k_tiles, n_k_tiles, lam_init):
    tq = q_ref.shape[1]
    j = pl.program_id(1)
    n_k = jnp.where(j < n_ctx_q_tiles, n_ctx_k_tiles, n_k_tiles)
    lqk = lqk_ref[...]
    lam = (jnp.exp(jnp.sum(lqk[0:1] * lqk[1:2], axis=1, keepdims=True))
           - jnp.exp(jnp.sum(lqk[2:3] * lqk[3:4], axis=1, keepdims=True)) + lam_init)
    row = _iota((LANES, tq), 0)
    dv = 2 * DIFF_D
    k_lanes, v_rows = [], []
    for grp in range(2 * DIFF_HEADS * DIFF_D // LANES):
        qgt = q_ref[0, :, grp * LANES:(grp + 1) * LANES].astype(F32).T
        for sub in range(LANES // DIFF_D):
            blk = grp * (LANES // DIFF_D) + sub
            keep = (row >= sub * DIFF_D) & (row < (sub + 1) * DIFF_D)
            qm_s[blk] = jnp.where(keep, qgt, 0.0).astype(BF16)
            k_lanes.append((grp * LANES, (grp + 1) * LANES))
            v_rows.append(((blk // 2) * VT_ROWS, (blk // 2 + 1) * VT_ROWS))
    _flash_chains(qm_s, k_ref, vt_ref, k_lanes, v_rows, n_k, n_ctx_k_tiles,
                  _tiles_per_iter(n_k_tiles - n_ctx_k_tiles, 2 * DIFF_HEADS), m_s, acc_s)
    outs = []
    for h in range(DIFF_HEADS):
        a1 = acc_s[2 * h]
        a2 = acc_s[2 * h + 1]
        o = a1[0:dv] / a1[dv:dv + 1] - lam * (a2[0:dv] / a2[dv:dv + 1])
        o = o * lax.rsqrt(jnp.mean(o * o, axis=0, keepdims=True) + LN_EPS)
        outs.append(o * g_ref[...] * (1.0 - lam_init))
    o_ref[0] = jnp.concatenate(outs, axis=0).T


def _diff_attn(q, k, vt, lqk, g_col, n_ctx, lam_init):
    b, s, _ = q.shape
    tq, tk = ATTN_Q_TILE, ATTN_K_TILE
    return pl.pallas_call(
        functools.partial(_diff_attn_body, n_ctx_q_tiles=n_ctx // tq, n_ctx_k_tiles=n_ctx // tk,
                          n_k_tiles=s // tk, lam_init=lam_init),
        grid=(b, s // tq),
        in_specs=[pl.BlockSpec((1, tq, BR_WIDTH), lambda i, j: (i, j, 0)),
                  pl.BlockSpec((1, s, BR_WIDTH), lambda i, j: (i, 0, 0)),
                  pl.BlockSpec((1, s // tk, DIFF_HEADS * VT_ROWS, tk), lambda i, j: (i, 0, 0, 0)),
                  pl.BlockSpec(lqk.shape, lambda i, j: (0, 0)),
                  pl.BlockSpec(g_col.shape, lambda i, j: (0, 0))],
        out_specs=pl.BlockSpec((1, tq, BR_WIDTH), lambda i, j: (i, j, 0)),
        out_shape=jax.ShapeDtypeStruct((b, s, BR_WIDTH), F32),
        scratch_shapes=[pltpu.VMEM((2 * DIFF_HEADS, LANES, tq), BF16),
                        pltpu.VMEM((2 * DIFF_HEADS, 1, tq), F32),
                        pltpu.VMEM((2 * DIFF_HEADS, VT_ROWS, tq), F32)],
        compiler_params=_cparams(("arbitrary", "arbitrary")),
        name="diff_attn",
    )(q, k, vt, lqk, g_col)


def _pair_stack(a, lane2):
    lo = jnp.where(lane2 < DN_DK, a, jnp.zeros_like(a))
    hi = jnp.where(lane2 >= DN_DK, a, jnp.zeros_like(a))
    return jnp.concatenate([lo, hi], axis=0)


def _pair_blockdiag(a, bdmask):
    return jnp.where(bdmask, jnp.concatenate([a, a], axis=0), 0.0)


def _dn_batch_per_step(b):
    return DN_BATCH_PER_STEP if b % DN_BATCH_PER_STEP == 0 else 1


def _dn_chunk_body(pd_ref, prev_ref, next_ref, pab_ref, cw_ref, alog_ref, dtb_ref,
                   u_ref, w_ref, qk_ref, qd_ref, kd_ref, el_ref,
                   qkv_s, g_s, beta_s, *, n_ctx_tiles, n_tiles):
    nb, tm = pd_ref.shape[0], pd_ref.shape[1]
    c = DN_CHUNK
    j = pl.program_id(1)
    has_prev = jnp.logical_and(j != 0, j != n_ctx_tiles)
    has_next = jnp.logical_and(j != n_ctx_tiles - 1, j != n_tiles - 1)
    row = _iota((tm, 1), 0)
    ones_bd = _block_ones(BR_WIDTH, DN_DK).astype(BF16)
    for bb in range(nb):
        x = pd_ref[bb]
        prev_row = jnp.where(has_prev, prev_ref[bb, SUBLANES - 1:SUBLANES, :], 0.0)
        next_row = jnp.where(has_next, next_ref[bb, 0:1, :], 0.0)
        x_m = jnp.where(row == 0, prev_row, pltpu.roll(x, 1, axis=0))
        x_p = jnp.where(row == tm - 1, next_row, pltpu.roll(x, tm - 1, axis=0))
        y = x_m * cw_ref[0:1, :] + x * cw_ref[1:2, :] + x_p * cw_ref[2:3, :]
        y = y * _sigmoid(y)
        q = y[:, 0:BR_WIDTH]
        k = y[:, BR_WIDTH:2 * BR_WIDTH]
        q = q * lax.rsqrt(_sum_dot(q * q, ones_bd, 2) + LN_EPS) * (DN_DK ** -0.5)
        k = k * lax.rsqrt(_sum_dot(k * k, ones_bd, 2) + LN_EPS)
        qkv_s[bb, :, 0:BR_WIDTH] = q
        qkv_s[bb, :, BR_WIDTH:2 * BR_WIDTH] = k
        qkv_s[bb, :, 2 * BR_WIDTH:3 * BR_WIDTH] = y[:, 2 * BR_WIDTH:3 * BR_WIDTH]
        ab = pab_ref[bb]
        z = ab + dtb_ref[...]
        softplus = jnp.maximum(z, 0.0) + jnp.log(1.0 + jnp.exp(-jnp.abs(z)))
        g_s[bb] = -jnp.exp(alog_ref[...]) * softplus
        beta_s[bb] = _sigmoid(ab)

    ii = _iota((c, LANES), 0)
    lane2 = _iota((c, LANES), 1)
    jj = lane2 % c
    eye2 = (ii == jj).astype(F32)
    bdmask = _iota((LANES, LANES), 0) // c == _iota((LANES, LANES), 1) // c
    ri = _iota((c, c), 0)
    ci = _iota((c, c), 1)
    tri_f = (ri >= ci).astype(F32).astype(BF16)
    tri_b = (ri <= ci).astype(F32).astype(BF16)
    first_half = lane2 < DN_DK
    nh = DN_HEADS

    def chunk_group(grp, carry):
        systems = []
        for bb, cc in [(bb, cc) for bb in range(nb) for cc in range(DN_CHUNKS_PER_GROUP)]:
            ch = grp * DN_CHUNKS_PER_GROUP + cc
            rows = pl.ds(pl.multiple_of(ch * c, c), c)
            el_rows = pl.ds(pl.multiple_of(ch * SUBLANES, SUBLANES), SUBLANES)
            gc = g_s[bb, rows, :]
            bc = beta_s[bb, rows, :]
            gam = jnp.where(lane2 < nh, _sel_dot(tri_f, gc), _sel_dot(tri_b, gc))
            gam_t = jnp.concatenate([gam, gam], axis=0).T
            for d in range(2):
                incl = (ii >= jj) if d == 0 else (ii <= jj)
                strict = (ii > jj) if d == 0 else (ii < jj)
                for p in range(nh // 2):
                    l0 = d * nh + 2 * p
                    l1 = l0 + 1
                    gcol = jnp.where(first_half, gam[:, l0:l0 + 1], gam[:, l1:l1 + 1])
                    grow = jnp.where(first_half, gam_t[l0:l0 + 1, :], gam_t[l1:l1 + 1, :])
                    bcol = jnp.where(first_half, bc[:, 2 * nh + l0:2 * nh + l0 + 1],
                                     bc[:, 2 * nh + l1:2 * nh + l1 + 1])
                    kg = qkv_s[bb, rows, BR_WIDTH + p * LANES:BR_WIDTH + (p + 1) * LANES]
                    systems.append(dict(
                        bb=bb, d=d, rows=rows, el_rows=el_rows, lanes=slice(p * LANES, (p + 1) * LANES),
                        strict=strict, gcol=gcol, bcol=bcol, kg=kg, kb=kg * bcol,
                        dec=jnp.where(incl, jnp.exp(jnp.where(incl, gcol - grow, 0.0)), 0.0),
                        qg=qkv_s[bb, rows, p * LANES:(p + 1) * LANES],
                        vg=qkv_s[bb, rows, 2 * BR_WIDTH + p * LANES:2 * BR_WIDTH + (p + 1) * LANES],
                        kstack=_pair_stack(kg, lane2).astype(BF16)))
        for sy in systems:
            both = _dot_nt(jnp.concatenate([sy["kb"], sy["qg"]], axis=0).astype(BF16), sy["kstack"])
            sy["a"] = jnp.where(sy["strict"], both[0:c] * sy["dec"], 0.0)
            qk_ref[sy["bb"], sy["d"], sy["rows"], sy["lanes"]] = (both[c:2 * c] * sy["dec"]).astype(BF16)
        bd = lambda a: _pair_blockdiag(a, bdmask).astype(BF16)
        for sy in systems:
            pm = -jnp.where(ii // DN_INV_BLOCK == jj // DN_INV_BLOCK, sy["a"], 0.0)
            sy["t"] = eye2 + pm
            sy["pm"] = _dot(pm.astype(BF16), bd(pm))
        for sy in systems:
            both = _dot(jnp.concatenate([sy["t"], sy["pm"]], axis=0).astype(BF16), bd(sy["pm"]))
            sy["t"] = sy["t"] + both[0:c]
            sy["pm"] = both[c:2 * c]
        for sy in systems:
            sy["t"] = sy["t"] + _dot(sy["t"].astype(BF16), bd(sy["pm"]))
        size = DN_INV_BLOCK
        while size < c:
            coupling = (ii // (2 * size) == jj // (2 * size)) & (ii // size != jj // size)
            for sy in systems:
                sy["x"] = _dot(jnp.where(coupling, sy["a"], 0.0).astype(BF16), bd(sy["t"]))
            for sy in systems:
                sy["t"] = sy["t"] - _dot(sy["t"].astype(BF16), bd(sy["x"]))
            size *= 2
        for sy in systems:
            sy["eg"] = jnp.exp(sy["gcol"])
            rhs = jnp.concatenate([_pair_stack(sy["vg"] * sy["bcol"], lane2),
                                   _pair_stack(sy["kb"] * sy["eg"], lane2)], axis=1).astype(BF16)
            uw = _dot(sy["t"].astype(BF16), rhs)
            u_ref[sy["bb"], sy["d"], sy["rows"], sy["lanes"]] = uw[:, 0:LANES]
            w_ref[sy["bb"], sy["d"], sy["rows"], sy["lanes"]] = uw[:, LANES:2 * LANES].astype(BF16)
        for sy in systems:
            gcol = sy["gcol"]
            glast = gcol[c - 1:c, :] if sy["d"] == 0 else gcol[0:1, :]
            qd_ref[sy["bb"], sy["d"], sy["rows"], sy["lanes"]] = (sy["qg"] * sy["eg"]).astype(BF16)
            kd_ref[sy["bb"], sy["d"], sy["rows"], sy["lanes"]] = (sy["kg"] * jnp.exp(glast - gcol)).astype(BF16)
            el_ref[sy["bb"], sy["d"], sy["el_rows"], sy["lanes"]] = jnp.broadcast_to(jnp.exp(glast), (SUBLANES, LANES))
        return carry

    lax.fori_loop(0, tm // (c * DN_CHUNKS_PER_GROUP), chunk_group, 0)


def _dn_chunk(pd, pab, conv_w, alog, dtb, n_ctx):
    b, s, _ = pd.shape
    tm = TOKEN_TILE
    nt = s // tm
    nsub = tm // SUBLANES
    last_blk = s // SUBLANES - 1
    full = lambda a: pl.BlockSpec(a.shape, lambda i, j: (0,) * a.ndim)
    nb = _dn_batch_per_step(b)
    dir_tok = lambda: pl.BlockSpec((nb, 2, tm, BR_WIDTH), lambda i, j: (i, 0, j, 0))
    dir_shape = lambda dt: jax.ShapeDtypeStruct((b, 2, s, BR_WIDTH), dt)
    el_rows = (tm // DN_CHUNK) * SUBLANES
    return pl.pallas_call(
        functools.partial(_dn_chunk_body, n_ctx_tiles=n_ctx // tm, n_tiles=nt),
        grid=(b // nb, nt),
        in_specs=[pl.BlockSpec((nb, tm, PD_COLS), lambda i, j: (i, j, 0)),
                  pl.BlockSpec((nb, SUBLANES, PD_COLS), lambda i, j: (i, jnp.maximum(j * nsub - 1, 0), 0)),
                  pl.BlockSpec((nb, SUBLANES, PD_COLS),
                               lambda i, j: (i, jnp.minimum((j + 1) * nsub, last_blk), 0)),
                  pl.BlockSpec((nb, tm, PAB_COLS), lambda i, j: (i, j, 0)),
                  full(conv_w), full(alog), full(dtb)],
        out_specs=[dir_tok(), dir_tok(), dir_tok(), dir_tok(), dir_tok(),
                   pl.BlockSpec((nb, 2, el_rows, BR_WIDTH), lambda i, j: (i, 0, j, 0))],
        out_shape=[dir_shape(F32)] + [dir_shape(BF16)] * 4
        + [jax.ShapeDtypeStruct((b, 2, nt * el_rows, BR_WIDTH), F32)],
        scratch_shapes=[pltpu.VMEM((nb, tm, PD_COLS), F32), pltpu.VMEM((nb, tm, PAB_COLS), F32),
                        pltpu.VMEM((nb, tm, PAB_COLS), F32)],
        compiler_params=_cparams(("arbitrary", "arbitrary")),
        name="dn_chunk",
    )(pd, pd, pd, pab, conv_w, alog, dtb)


def _dn_scan_body(uf, wf, qkf, qdf, kdf, elf, ub, wb, qkb, qdb, kdb, elb, of_ref, ob_ref, s_ref):
    nb, tm = uf.shape[0], uf.shape[2]
    c = DN_CHUNK
    n_chunks = tm // c

    @pl.when(pl.program_id(1) == 0)
    def _():
        s_ref[...] = jnp.zeros_like(s_ref)

    bdmask = _iota((LANES, LANES), 0) // c == _iota((LANES, LANES), 1) // c
    views = ((uf, wf, qkf, qdf, kdf, elf, of_ref), (ub, wb, qkb, qdb, kdb, elb, ob_ref))
    chains = [(bb, d, p) for bb in range(nb) for d in range(2) for p in range(DN_HEADS // 2)]
    state = {ch: s_ref[ch] for ch in chains}
    for step in range(n_chunks):
        blk = {}
        for bb, d, p in chains:
            ck = step if d == 0 else n_chunks - 1 - step
            blk[bb, d, p] = (slice(ck * c, (ck + 1) * c), slice(p * LANES, (p + 1) * LANES), ck)
        tile = lambda k, ch: views[ch[1]][k][ch[0], 0, blk[ch][0], blk[ch][1]]
        sb = {ch: state[ch].astype(BF16) for ch in chains}
        ws = {ch: _dot(tile(1, ch), sb[ch]) for ch in chains}
        qs = {ch: _dot(tile(3, ch), sb[ch]) for ch in chains}
        vb = {ch: (tile(0, ch) - ws[ch]).astype(BF16) for ch in chains}
        for ch in chains:
            vbd = jnp.where(bdmask, jnp.concatenate([vb[ch], vb[ch]], axis=0), jnp.zeros((), BF16))
            views[ch[1]][6][ch[0], blk[ch][0], blk[ch][1]] = qs[ch] + _dot(tile(2, ch), vbd)
        upd = {ch: _dot_tn(tile(4, ch), vb[ch]) for ch in chains}
        for ch in chains:
            ck = blk[ch][2]
            el = views[ch[1]][5][ch[0], 0, ck * SUBLANES:ck * SUBLANES + 1, blk[ch][1]]
            state[ch] = state[ch] * el + jnp.where(bdmask, upd[ch], 0.0)
    for ch in chains:
        s_ref[ch] = state[ch]


def _dn_scan(u, w, qk, qd, kd, el, n_ctx):
    b, _, s, _ = u.shape
    tm = TOKEN_TILE
    nb = _dn_batch_per_step(b)
    nt = s // tm
    nctx = n_ctx // tm
    el_rows = (tm // DN_CHUNK) * SUBLANES

    def rev(j):
        return jnp.where(j < nctx, nctx - 1 - j, nt - 1 - (j - nctx))

    fwd = lambda rows: pl.BlockSpec((nb, 1, rows, BR_WIDTH), lambda i, j: (i, 0, j, 0))
    bwd = lambda rows: pl.BlockSpec((nb, 1, rows, BR_WIDTH), lambda i, j: (i, 1, rev(j), 0))
    in_specs = [fwd(tm)] * 5 + [fwd(el_rows)] + [bwd(tm)] * 5 + [bwd(el_rows)]
    out_shape = jax.ShapeDtypeStruct((b, s, BR_WIDTH), F32)
    return pl.pallas_call(
        _dn_scan_body,
        grid=(b // nb, nt),
        in_specs=in_specs,
        out_specs=[pl.BlockSpec((nb, tm, BR_WIDTH), lambda i, j: (i, j, 0)),
                   pl.BlockSpec((nb, tm, BR_WIDTH), lambda i, j: (i, rev(j), 0))],
        out_shape=[out_shape, out_shape],
        scratch_shapes=[pltpu.VMEM((nb, 2, DN_HEADS // 2, LANES, LANES), F32)],
        compiler_params=_cparams(("arbitrary", "arbitrary")),
        name="dn_scan",
    )(u, w, qk, qd, kd, el, u, w, qk, qd, kd, el)


def _merge_body(x_ref, mod_ref, h_ref, pg_ref, ya_ref, yb_ref, yc_ref, of_ref, ob_ref, dng_ref,
                wg_ref, wb_ref, wo_ref, lng_ref, lnb_ref, o_ref, *, alpha):
    hb = h_ref[0]
    o = of_ref[0] + ob_ref[0]
    ms = _sum_dot(o * o, _block_ones(BR_WIDTH, DN_DV).astype(BF16), 2) * (1.0 / DN_DV)
    yd = o * lax.rsqrt(ms + LN_EPS) * dng_ref[...]
    pg = pg_ref[0]
    sg = pg * _sigmoid(pg)
    acc = None
    for i, y in enumerate((ya_ref[0], yb_ref[0], yc_ref[0], yd)):
        t = (y * sg[:, i * BR_WIDTH:(i + 1) * BR_WIDTH]).astype(BF16)
        term = _sigmoid(_dot(hb, wg_ref[i])) * _dot(t, wb_ref[i])
        acc = term if acc is None else acc + term
    out = _dot(acc.astype(BF16), wo_ref[...])
    gt = mod_ref[0, 0, 2:3, :]
    r = alpha * x_ref[0] + gt * out
    mu = jnp.mean(r, -1, keepdims=True)
    rc = r - mu
    var = jnp.mean(rc * rc, -1, keepdims=True)
    o_ref[0] = rc * lax.rsqrt(var + LN_EPS) * lng_ref[...] + lnb_ref[...]


def _merge(xs, mod, h, pg, ya, yb, yc, o_f, o_b, dng, wg, wb, wo, lng, lnb, n_ctx_tiles, skip_tiles, alpha):
    b, s, d = xs.shape
    tm = TOKEN_TILE
    nt = s // tm - skip_tiles
    tok = lambda width: pl.BlockSpec((1, tm, width), lambda i, j: (i, j + skip_tiles, 0))
    full = lambda a: pl.BlockSpec(a.shape, lambda i, j: (0,) * a.ndim)
    return pl.pallas_call(
        functools.partial(_merge_body, alpha=alpha),
        grid=(b, nt),
        in_specs=[tok(d),
                  pl.BlockSpec((1, 1, 3, d),
                               lambda i, j: (i, jnp.where(j + skip_tiles >= n_ctx_tiles, 1, 0), 0, 0)),
                  tok(d), tok(PG_COLS), tok(BR_WIDTH), tok(BR_WIDTH), tok(BR_WIDTH), tok(BR_WIDTH),
                  tok(BR_WIDTH), full(dng), full(wg), full(wb), full(wo), full(lng), full(lnb)],
        out_specs=pl.BlockSpec((1, tm, d), lambda i, j: (i, j, 0)),
        out_shape=jax.ShapeDtypeStruct((b, nt * tm, d), F32),
        compiler_params=_cparams(("arbitrary", "arbitrary")),
        name="merge",
    )(xs, mod, h, pg, ya, yb, yc, o_f, o_b, dng, wg, wb, wo, lng, lnb)


def _rope_swap_index(n_blocks):
    blk = jnp.array(list(range(8, 16)) + list(range(0, 8)) + list(range(24, 32)) + list(range(16, 24)))
    return (jnp.arange(n_blocks)[:, None] * 32 + blk[None, :]).reshape(-1)


def _pack_w_in(w):
    d = w.shape[0]
    o = 0
    cq = w[:, o:o + MLA_Q_LORA]; o += MLA_Q_LORA
    ckv = w[:, o:o + MLA_KV_LORA]; o += MLA_KV_LORA
    kr = w[:, o:o + MLA_ROPE]; o += MLA_ROPE
    pb = w[:, o:o + GMLP_COLS]; o += GMLP_COLS
    dq = w[:, o:o + 256]; dk = w[:, o + 256:o + 512]; dv = w[:, o + 512:o + 768]; o += DIFF_COLS
    dn_qkv = w[:, o:o + 3 * BR_WIDTH]; dn_ab = w[:, o + 3 * BR_WIDTH:o + DN_COLS]; o += DN_COLS
    pg = w[:, o:]
    z = lambda n: jnp.zeros((d, n), w.dtype)
    sw1 = _rope_swap_index(1)
    sw8 = _rope_swap_index(8)
    place = lambda a: jnp.concatenate([z(MLA_NOPE), a, z(LANES - MLA_NOPE - MLA_ROPE)], axis=1)
    packed = jnp.concatenate(
        [cq, ckv, place(kr), place(kr[:, sw1]), pb, dq, dk, dv, dq[:, sw8], dk[:, sw8], dn_qkv,
         dn_ab, z(PAB_COLS - 4 * DN_HEADS), pg], axis=1)
    return packed.astype(BF16)


def _pack_mla_weights(w_uq, w_ukv):
    dq = MLA_NOPE + MLA_ROPE
    wq = w_uq.reshape(MLA_Q_LORA, MLA_HEADS, dq)
    zq = jnp.zeros((MLA_Q_LORA, MLA_HEADS, LANES - dq), w_uq.dtype)
    wq_p = jnp.concatenate([wq, zq], axis=2).reshape(MLA_Q_LORA, MLA_HEADS * LANES)
    rope_sw = wq[:, :, MLA_NOPE:][:, :, _rope_swap_index(1)]
    wqs_p = jnp.concatenate([jnp.zeros((MLA_Q_LORA, MLA_HEADS, MLA_NOPE), w_uq.dtype), rope_sw, zq],
                            axis=2).reshape(MLA_Q_LORA, MLA_HEADS * LANES)
    wkv = w_ukv.reshape(MLA_KV_LORA, MLA_HEADS, MLA_NOPE + MLA_V)
    zk = jnp.zeros((MLA_KV_LORA, MLA_HEADS, LANES - MLA_NOPE), w_ukv.dtype)
    wk_p = jnp.concatenate([wkv[:, :, :MLA_NOPE], zk], axis=2).reshape(MLA_KV_LORA, MLA_HEADS * LANES)
    wv_p = wkv[:, :, MLA_NOPE:].reshape(MLA_KV_LORA, MLA_HEADS * MLA_V)
    return wq_p.astype(BF16), wqs_p.astype(BF16), wk_p.astype(BF16), wv_p.astype(BF16)


def _rope_tables(n, n_ctx):
    rows = n // GRID_W
    row = jnp.repeat(jnp.arange(rows, dtype=F32), GRID_W)
    col = jnp.tile(jnp.arange(GRID_W, dtype=F32), rows)
    axis_dim = MLA_ROPE // 2
    inv_freq = ROPE_BASE ** (-jnp.arange(0, axis_dim, 2, dtype=F32) / axis_dim)
    ar = row[:, None] * inv_freq
    ac = col[:, None] * inv_freq
    cos32 = jnp.concatenate([jnp.cos(ar), jnp.cos(ar), jnp.cos(ac), jnp.cos(ac)], axis=1)
    sin32 = jnp.concatenate([-jnp.sin(ar), jnp.sin(ar), -jnp.sin(ac), jnp.sin(ac)], axis=1)
    cos32 = jnp.concatenate([jnp.ones((n_ctx, 32), F32), cos32], axis=0)
    sin32 = jnp.concatenate([jnp.zeros((n_ctx, 32), F32), sin32], axis=0)
    s = n + n_ctx
    one = lambda w: jnp.ones((s, w), F32)
    zero = lambda w: jnp.zeros((s, w), F32)
    mla_cos = jnp.concatenate([one(MLA_NOPE), cos32, one(LANES - MLA_NOPE - MLA_ROPE)], axis=1)
    mla_sin = jnp.concatenate([zero(MLA_NOPE), sin32, zero(LANES - MLA_NOPE - MLA_ROPE)], axis=1)
    diff_cos = jnp.tile(cos32, (1, LANES // 32))
    diff_sin = jnp.tile(sin32, (1, LANES // 32))
    return mla_cos, mla_sin, diff_cos, diff_sin


def _pad_lanes(a, width=LANES):
    return jnp.concatenate([a, jnp.zeros(a.shape[:-1] + (width - a.shape[-1],), a.dtype)], axis=-1)


def kernel(x, c, ctx, c_ctx, w_mod, b_mod, w_in, mla_q_norm, mla_w_uq, mla_kv_norm, mla_w_ukv, gmlp_ln_g, gmlp_w_s, gmlp_b_s, diff_lq1, diff_lk1, diff_lq2, diff_lk2, diff_norm_g, dn_conv_w, dn_a_log, dn_dt_bias, dn_norm_g, w_gate, w_branch, w_out, ln_g, ln_b):
    b, n, d = x.shape
    n_ctx = ctx.shape[1]
    depth = w_mod.shape[0]
    tm = TOKEN_TILE
    assert d == D_MODEL and n % tm == 0 and n_ctx % tm == 0 and n % GRID_W == 0
    assert tm == ATTN_Q_TILE == ATTN_K_TILE
    alpha = (2 * depth) ** 0.25
    n_ctx_tiles = n_ctx // tm

    xs = jnp.concatenate([ctx, x], axis=1)
    mla_cos, mla_sin, diff_cos, diff_sin = _rope_tables(n, n_ctx)

    rows = ((b + 1 + SUBLANES - 1) // SUBLANES) * SUBLANES
    cc = jnp.concatenate([c, c_ctx[None, :], jnp.zeros((rows - b - 1, d), F32)], axis=0)
    mod_all = _modulation(cc, w_mod, b_mod)

    for l in range(depth):
        last = l == depth - 1
        lam_init = 0.8 - 0.6 * math.exp(-0.3 * l)
        mod_l = mod_all[l].reshape(rows, 3, d)
        mod = jnp.stack([jnp.broadcast_to(mod_l[b][None], (b, 3, d)), mod_l[:b]], axis=1)

        wq, wqs, wk, wv = _pack_mla_weights(mla_w_uq[l], mla_w_ukv[l])
        bias = jnp.repeat(gmlp_b_s[l].T, BR_WIDTH // GMLP_GROUPS, axis=1)
        h, pd, pab, pg, q_a, k_a, vt_a, yb, q_c, k_c, vt_c = _inproj(
            xs, mod, _pack_w_in(w_in[l]), (mla_cos, mla_sin, diff_cos, diff_sin),
            (mla_q_norm[l][None, :], mla_kv_norm[l][None, :], wq, wqs, wk, wv),
            (gmlp_ln_g[l][None, :], gmlp_w_s[l].astype(BF16), bias), n_ctx_tiles)
        ya = _mla_attn(q_a, k_a, vt_a, n_ctx)
        lqk = _pad_lanes(jnp.stack([diff_lq1[l], diff_lk1[l], diff_lq2[l], diff_lk2[l]], axis=0))
        yc = _diff_attn(q_c, k_c, vt_c, lqk, diff_norm_g[l][:, None], n_ctx, lam_init)

        conv_w = jnp.concatenate([dn_conv_w[l], jnp.zeros((SUBLANES - 3, PD_COLS), F32)], axis=0)
        alog = _pad_lanes(dn_a_log[l].reshape(1, 2 * DN_HEADS))
        dtb = _pad_lanes(dn_dt_bias[l].reshape(1, 2 * DN_HEADS))
        u, w, qk, qd, kd, el = _dn_chunk(pd, pab, conv_w, alog, dtb, n_ctx)
        o_f, o_b = _dn_scan(u, w, qk, qd, kd, el, n_ctx)

        dng = jnp.tile(dn_norm_g[l], DN_HEADS)[None, :]
        xs = _merge(xs, mod, h, pg, ya, yb, yc, o_f, o_b, dng, w_gate[l].astype(BF16),
                    w_branch[l].astype(BF16), w_out[l].astype(BF16), ln_g[l][None, :], ln_b[l][None, :],
                    n_ctx_tiles, n_ctx_tiles if last else 0, alpha)
    return xs
```

```python
import functools
import math

import jax
import jax.numpy as jnp
from jax import lax
from jax.experimental import pallas as pl
from jax.experimental.pallas import tpu as pltpu

F32 = jnp.float32
BF16 = jnp.bfloat16

D_MODEL = 1024
GRID_W = 64
N_BRANCH = 4
BR_WIDTH = 256
MLA_HEADS = 4
MLA_NOPE = 64
MLA_ROPE = 32
MLA_V = 64
MLA_Q_LORA = 256
MLA_KV_LORA = 128
GMLP_GROUPS = 4
GMLP_CHUNK = 128
DIFF_HEADS = 4
DIFF_D = 32
DN_HEADS = 4
DN_DK = 64
DN_DV = 64
DN_CHUNK = 64
ROPE_BASE = 10000.0
LN_EPS = 1e-6

MLA_COLS = MLA_Q_LORA + MLA_KV_LORA + MLA_ROPE
GMLP_COLS = 2 * BR_WIDTH
DIFF_COLS = 3 * DIFF_HEADS * 2 * DIFF_D
DN_COLS = 3 * BR_WIDTH + 4 * DN_HEADS

LANES = 128
SUBLANES = 8
VMEM_LIMIT_BYTES = 56 * 1024 * 1024

TOKEN_TILE = 256
ATTN_Q_TILE = 256
ATTN_K_TILE = 256
DN_CHUNKS_PER_GROUP = 4
DN_BATCH_PER_STEP = 2
DN_SCAN_BATCH_PER_STEP = 4
DN_INV_BLOCK = 8
ATTN_STEPS_PER_TRIP = 128
ATTN_LOOKAHEAD = 5
NEG_BIG = -1e30
VT_ROWS = 80
LOG2_E = 1.4426950408889634

PA_COLS = 640
PB_COLS = 512
PC_COLS = 1280
PD_COLS = 768
PAB_COLS = 128
PG_COLS = 1024
PACK_SPLITS = (PA_COLS, PB_COLS, PC_COLS, PD_COLS, PAB_COLS, PG_COLS)
PACK_COLS = sum(PACK_SPLITS)


def _cparams(semantics):
    return pltpu.CompilerParams(dimension_semantics=semantics, vmem_limit_bytes=VMEM_LIMIT_BYTES)


def _dot(a, b):
    return jnp.dot(a, b, preferred_element_type=F32)


def _dot_nt(a, b):
    return lax.dot_general(a, b, (((1,), (1,)), ((), ())), preferred_element_type=F32)


def _dot_tn(a, b):
    return lax.dot_general(a, b, (((0,), (0,)), ((), ())), preferred_element_type=F32)


def _split_bf16(x, terms):
    pieces = []
    for _ in range(terms):
        hi = x.astype(BF16)
        pieces.append(hi)
        x = x - hi.astype(F32)
    return pieces


def _sum_dot(x, sel, terms=3):
    return sum(_dot(p, sel) for p in _split_bf16(x, terms))


def _sel_dot(sel, x, terms=3):
    return sum(_dot(sel, p) for p in _split_bf16(x, terms))


def _sigmoid(x):
    return 1.0 / (1.0 + jnp.exp(-x))


def _iota(shape, axis):
    return lax.broadcasted_iota(jnp.int32, shape, axis)


def _block_ones(n, blk):
    return (_iota((n, n), 0) // blk == _iota((n, n), 1) // blk).astype(F32)


def _mod_body(c_ref, w_ref, b_ref, o_ref):
    c = c_ref[...]
    s = (c * _sigmoid(c)).astype(BF16)
    o_ref[0] = _dot(s, w_ref[0].astype(BF16)) + b_ref[0]


def _modulation(cc, w_mod, b_mod):
    depth, d, n3 = w_mod.shape
    rows = cc.shape[0]
    tn = 1024
    return pl.pallas_call(
        _mod_body,
        grid=(depth, n3 // tn),
        in_specs=[pl.BlockSpec((rows, d), lambda l, j: (0, 0)),
                  pl.BlockSpec((1, d, tn), lambda l, j: (l, 0, j)),
                  pl.BlockSpec((1, 1, tn), lambda l, j: (l, 0, j))],
        out_specs=pl.BlockSpec((1, rows, tn), lambda l, j: (l, 0, j)),
        out_shape=jax.ShapeDtypeStruct((depth, rows, n3), F32),
        compiler_params=_cparams(("arbitrary", "arbitrary")),
        name="modulation",
    )(cc, w_mod, b_mod.reshape(depth, 1, n3))


def _rms(x, g):
    return x * lax.rsqrt(jnp.mean(x * x, -1, keepdims=True) + LN_EPS) * g


def _mla_prep_values(pa, cos, sin, qn, kvn, wq, wqs, wk, wv, scale):
    cq = pa[:, 0:MLA_Q_LORA]
    ckv = pa[:, MLA_Q_LORA:MLA_Q_LORA + MLA_KV_LORA]
    kr = pa[:, 384:512]
    kr_sw = pa[:, 512:640]
    cos4 = jnp.concatenate([cos] * MLA_HEADS, axis=1)
    sin4 = jnp.concatenate([sin] * MLA_HEADS, axis=1)
    cqn = _rms(cq, qn).astype(BF16)
    q = ((_dot(cqn, wq) * cos4 + _dot(cqn, wqs) * sin4) * scale).astype(BF16)
    ckvn = _rms(ckv, kvn).astype(BF16)
    kr_rot = kr * cos + kr_sw * sin
    k = (_dot(ckvn, wk) + jnp.concatenate([kr_rot] * MLA_HEADS, axis=1)).astype(BF16)
    return q, k, _vt_with_ones(_dot(ckvn, wv), MLA_HEADS, MLA_V)


def _diff_prep_values(pc, cos, sin, scale):
    cos = jnp.concatenate([cos] * 2, axis=1)
    sin = jnp.concatenate([sin] * 2, axis=1)
    q = ((pc[:, 0:256] * cos + pc[:, 768:1024] * sin) * scale).astype(BF16)
    k = (pc[:, 256:512] * cos + pc[:, 1024:1280] * sin).astype(BF16)
    return q, k, _vt_with_ones(pc[:, 512:768], DIFF_HEADS, 2 * DIFF_D)


def _gelu_tanh(x):
    return 0.5 * x * (1.0 + jnp.tanh(math.sqrt(2.0 / math.pi) * (x + 0.044715 * (x * x * x))))


def _gmlp_values(pb, g, ws_ref, bias_ref, o_ref):
    z = _gelu_tanh(pb)
    u = z[:, 0:BR_WIDTH]
    v = z[:, BR_WIDTH:2 * BR_WIDTH]
    mu = jnp.mean(v, -1, keepdims=True)
    vc = v - mu
    var = jnp.mean(vc * vc, -1, keepdims=True)
    v = (vc * lax.rsqrt(var + LN_EPS) * g).astype(BF16)
    lane = _iota((GMLP_CHUNK, LANES), 1)
    gw = BR_WIDTH // GMLP_GROUPS
    for c in range(pb.shape[0] // GMLP_CHUNK):
        rows = slice(c * GMLP_CHUNK, (c + 1) * GMLP_CHUNK)
        for half in range(BR_WIDTH // LANES):
            cols = slice(half * LANES, (half + 1) * LANES)
            vch = v[rows, cols]
            a0 = _dot(ws_ref[2 * half], vch)
            a1 = _dot(ws_ref[2 * half + 1], vch)
            mixed = jnp.where(lane < gw, a0, a1) + bias_ref[:, cols]
            o_ref[0, rows, cols] = u[rows, cols] * mixed


def _inproj_body(x_ref, mod_ref, w_ref, cosm_ref, sinm_ref, cosd_ref, sind_ref, qn_ref, kvn_ref,
                 wq_ref, wqs_ref, wk_ref, wv_ref, gg_ref, ws_ref, bias_ref,
                 h_ref, pd_ref, pab_ref, pg_ref, qa_ref, ka_ref, vta_ref, yb_ref, qc_ref, kc_ref, vtc_ref,
                 *, mla_scale, diff_scale):
    x = x_ref[0]
    mu = jnp.mean(x, -1, keepdims=True)
    xc = x - mu
    var = jnp.mean(xc * xc, -1, keepdims=True)
    xn = xc * lax.rsqrt(var + LN_EPS)
    sh = mod_ref[0, 0, 0:1, :]
    sc = mod_ref[0, 0, 1:2, :]
    hb = (xn * (1.0 + sc) + sh).astype(BF16)
    h_ref[0] = hb
    offs = [sum(PACK_SPLITS[:i]) for i in range(len(PACK_SPLITS))]
    proj = lambda i: _dot(hb, w_ref[:, offs[i]:offs[i] + PACK_SPLITS[i]])
    pa = proj(0)
    pb = proj(1)
    pc = proj(2)
    qa_ref[0], ka_ref[0], vta_ref[0, 0] = _mla_prep_values(
        pa, cosm_ref[...], sinm_ref[...], qn_ref[...], kvn_ref[...], wq_ref[...], wqs_ref[...],
        wk_ref[...], wv_ref[...], mla_scale)
    pd_ref[0] = proj(3)
    _gmlp_values(pb, gg_ref[...], ws_ref, bias_ref, yb_ref)
    pg_ref[0] = proj(5)
    qc_ref[0], kc_ref[0], vtc_ref[0, 0] = _diff_prep_values(pc, cosd_ref[...], sind_ref[...], diff_scale)
    pab_ref[0] = proj(4)


def _inproj(xs, mod, w_pack, tables, mla_params, gmlp_params, n_ctx_tiles):
    b, s, d = xs.shape
    tm = TOKEN_TILE
    hp = MLA_HEADS * LANES
    tok = lambda width, dt: jax.ShapeDtypeStruct((b, s, width), dt)
    tok_spec = lambda width: pl.BlockSpec((1, tm, width), lambda i, j: (i, j, 0))
    full = lambda a: pl.BlockSpec(a.shape, lambda i, j: (0,) * a.ndim)
    table = pl.BlockSpec((tm, LANES), lambda i, j: (j, 0))
    vt_spec = pl.BlockSpec((1, 1, 4 * VT_ROWS, tm), lambda i, j: (i, j, 0, 0))
    vt_shape = jax.ShapeDtypeStruct((b, s // tm, 4 * VT_ROWS, tm), BF16)
    consts = list(mla_params) + list(gmlp_params)
    return pl.pallas_call(
        functools.partial(_inproj_body, mla_scale=(MLA_NOPE + MLA_ROPE) ** -0.5 * LOG2_E,
                          diff_scale=DIFF_D ** -0.5 * LOG2_E),
        grid=(b, s // tm),
        in_specs=[tok_spec(d),
                  pl.BlockSpec((1, 1, 3, d), lambda i, j: (i, jnp.where(j >= n_ctx_tiles, 1, 0), 0, 0)),
                  pl.BlockSpec((d, PACK_COLS), lambda i, j: (0, 0))]
        + [table] * 4 + [full(a) for a in consts],
        out_specs=[tok_spec(d), tok_spec(PD_COLS), tok_spec(PAB_COLS), tok_spec(PG_COLS),
                   tok_spec(hp), tok_spec(hp), vt_spec, tok_spec(BR_WIDTH),
                   tok_spec(BR_WIDTH), tok_spec(BR_WIDTH), vt_spec],
        out_shape=[tok(d, BF16), tok(PD_COLS, F32), tok(PAB_COLS, F32), tok(PG_COLS, F32),
                   tok(hp, BF16), tok(hp, BF16), vt_shape, tok(BR_WIDTH, F32),
                   tok(BR_WIDTH, BF16), tok(BR_WIDTH, BF16), vt_shape],
        compiler_params=_cparams(("arbitrary", "arbitrary")),
        name="inproj",
    )(xs, mod, w_pack, *tables, *consts)


def _vt_with_ones(v, heads, dv):
    tm = v.shape[0]
    vt = v.T
    aug = (_iota((VT_ROWS - dv, tm), 0) == 0).astype(F32)
    pieces = []
    for h in range(heads):
        pieces += [vt[h * dv:(h + 1) * dv], aug]
    return jnp.concatenate(pieces, axis=0).astype(BF16)


def _tiles_per_iter(n_tiles, n_chain):
    t = max(1, ATTN_STEPS_PER_TRIP // n_chain)
    while n_tiles % t:
        t -= 1
    return t


def _flash_chains(qt_s, k_ref, vt_ref, k_lanes, v_rows, n_k, n_first, tiles_per_iter, m_s, acc_s):
    tk = ATTN_K_TILE
    m_s[...] = jnp.full(m_s.shape, NEG_BIG, F32)
    acc_s[...] = jnp.zeros(acc_s.shape, F32)

    n_chain = len(k_lanes)

    def scores(c, kt):
        k_rows = pl.ds(pl.multiple_of(kt * tk, tk), tk)
        return _dot(k_ref[0, k_rows, k_lanes[c][0]:k_lanes[c][1]], qt_s[c])

    def run(kt0, n_iter, tiles_per_iter):
        per_iter = tiles_per_iter * n_chain

        def step(it, ahead):
            ahead = list(ahead)
            base = kt0 + it * tiles_per_iter
            for idx in range(per_iter):
                kt, c = base + idx // n_chain, idx % n_chain
                s = ahead.pop(0)
                nxt = idx + ATTN_LOOKAHEAD
                ahead.append(scores(nxt % n_chain, jnp.minimum(base + nxt // n_chain, n_k - 1)))
                m_old = m_s[c]
                m_new = jnp.maximum(m_old, jnp.max(s, axis=0, keepdims=True))
                p = jnp.exp2(s - m_new).astype(BF16)
                vt = vt_ref[0, kt, v_rows[c][0]:v_rows[c][1], :]
                acc_s[c] = jnp.exp2(m_old - m_new) * acc_s[c] + _dot(vt, p)
                m_s[c] = m_new
            return tuple(ahead)

        init = tuple(scores(i % n_chain, jnp.minimum(kt0 + i // n_chain, n_k - 1))
                     for i in range(ATTN_LOOKAHEAD))
        lax.fori_loop(0, n_iter, step, init)

    run(0, n_first, 1)
    run(n_first, (n_k - n_first) // tiles_per_iter, tiles_per_iter)


def _mla_attn_body(q_ref, k_ref, vt_ref, o_ref, qt_s, m_s, acc_s, *, n_ctx_q_tiles, n_ctx_k_tiles,
                   n_k_tiles):
    j = pl.program_id(1)
    n_k = jnp.where(j < n_ctx_q_tiles, n_ctx_k_tiles, n_k_tiles)
    k_lanes = [(h * LANES, (h + 1) * LANES) for h in range(MLA_HEADS)]
    v_rows = [(h * VT_ROWS, (h + 1) * VT_ROWS) for h in range(MLA_HEADS)]
    for h in range(MLA_HEADS):
        qt_s[h] = q_ref[0, :, h * LANES:(h + 1) * LANES].astype(F32).T.astype(BF16)
    _flash_chains(qt_s, k_ref, vt_ref, k_lanes, v_rows, n_k, n_ctx_k_tiles,
                  _tiles_per_iter(n_k_tiles - n_ctx_k_tiles, MLA_HEADS), m_s, acc_s)
    outs = []
    for h in range(MLA_HEADS):
        acc = acc_s[h]
        outs.append(acc[0:MLA_V] / acc[MLA_V:MLA_V + 1])
    o_ref[0] = jnp.concatenate(outs, axis=0).T


def _mla_attn(q, k, vt, n_ctx):
    b, s, hp = q.shape
    tq, tk = ATTN_Q_TILE, ATTN_K_TILE
    return pl.pallas_call(
        functools.partial(_mla_attn_body, n_ctx_q_tiles=n_ctx // tq, n_ctx_k_tiles=n_ctx // tk,
                          n_k_tiles=s // tk),
        grid=(b, s // tq),
        in_specs=[pl.BlockSpec((1, tq, hp), lambda i, j: (i, j, 0)),
                  pl.BlockSpec((1, s, hp), lambda i, j: (i, 0, 0)),
                  pl.BlockSpec((1, s // tk, MLA_HEADS * VT_ROWS, tk), lambda i, j: (i, 0, 0, 0))],
        out_specs=pl.BlockSpec((1, tq, BR_WIDTH), lambda i, j: (i, j, 0)),
        out_shape=jax.ShapeDtypeStruct((b, s, BR_WIDTH), F32),
        scratch_shapes=[pltpu.VMEM((MLA_HEADS, LANES, tq), BF16), pltpu.VMEM((MLA_HEADS, 1, tq), F32),
                        pltpu.VMEM((MLA_HEADS, VT_ROWS, tq), F32)],
        compiler_params=_cparams(("arbitrary", "arbitrary")),
        name="mla_attn",
    )(q, k, vt)


def _diff_attn_body(q_ref, k_ref, vt_ref, lqk_ref, g_ref, o_ref, qm_s, m_s, acc_s, *, n_ctx_q_tiles,
                    n_ctx_k_tiles, n_k_tiles, lam_init):
    tq = q_ref.shape[1]
    j = pl.program_id(1)
    n_k = jnp.where(j < n_ctx_q_tiles, n_ctx_k_tiles, n_k_tiles)
    lqk = lqk_ref[...]
    lam = (jnp.exp(jnp.sum(lqk[0:1] * lqk[1:2], axis=1, keepdims=True))
           - jnp.exp(jnp.sum(lqk[2:3] * lqk[3:4], axis=1, keepdims=True)) + lam_init)
    row = _iota((LANES, tq), 0)
    dv = 2 * DIFF_D
    k_lanes, v_rows = [], []
    for grp in range(2 * DIFF_HEADS * DIFF_D // LANES):
        qgt = q_ref[0, :, grp * LANES:(grp + 1) * LANES].astype(F32).T
        for sub in range(LANES // DIFF_D):
            blk = grp * (LANES // DIFF_D) + sub
            keep = (row >= sub * DIFF_D) & (row < (sub + 1) * DIFF_D)
            qm_s[blk] = jnp.where(keep, qgt, 0.0).astype(BF16)
            k_lanes.append((grp * LANES, (grp + 1) * LANES))
            v_rows.append(((blk // 2) * VT_ROWS, (blk // 2 + 1) * VT_ROWS))
    _flash_chains(qm_s, k_ref, vt_ref, k_lanes, v_rows, n_k, n_ctx_k_tiles,
                  _tiles_per_iter(n_k_tiles - n_ctx_k_tiles, 2 * DIFF_HEADS), m_s, acc_s)
    outs = []
    for h in range(DIFF_HEADS):
        a1 = acc_s[2 * h]
        a2 = acc_s[2 * h + 1]
        o = a1[0:dv] / a1[dv:dv + 1] - lam * (a2[0:dv] / a2[dv:dv + 1])
        o = o * lax.rsqrt(jnp.mean(o * o, axis=0, keepdims=True) + LN_EPS)
        outs.append(o * g_ref[...] * (1.0 - lam_init))
    o_ref[0] = jnp.concatenate(outs, axis=0).T


def _diff_attn(q, k, vt, lqk, g_col, n_ctx, lam_init):
    b, s, _ = q.shape
    tq, tk = ATTN_Q_TILE, ATTN_K_TILE
    return pl.pallas_call(
        functools.partial(_diff_attn_body, n_ctx_q_tiles=n_ctx // tq, n_ctx_k_tiles=n_ctx // tk,
                          n_k_tiles=s // tk, lam_init=lam_init),
        grid=(b, s // tq),
        in_specs=[pl.BlockSpec((1, tq, BR_WIDTH), lambda i, j: (i, j, 0)),
                  pl.BlockSpec((1, s, BR_WIDTH), lambda i, j: (i, 0, 0)),
                  pl.BlockSpec((1, s // tk, DIFF_HEADS * VT_ROWS, tk), lambda i, j: (i, 0, 0, 0)),
                  pl.BlockSpec(lqk.shape, lambda i, j: (0, 0)),
                  pl.BlockSpec(g_col.shape, lambda i, j: (0, 0))],
        out_specs=pl.BlockSpec((1, tq, BR_WIDTH), lambda i, j: (i, j, 0)),
        out_shape=jax.ShapeDtypeStruct((b, s, BR_WIDTH), F32),
        scratch_shapes=[pltpu.VMEM((2 * DIFF_HEADS, LANES, tq), BF16),
                        pltpu.VMEM((2 * DIFF_HEADS, 1, tq), F32),
                        pltpu.VMEM((2 * DIFF_HEADS, VT_ROWS, tq), F32)],
        compiler_params=_cparams(("arbitrary", "arbitrary")),
        name="diff_attn",
    )(q, k, vt, lqk, g_col)


def _pair_stack(a, lane2):
    lo = jnp.where(lane2 < DN_DK, a, jnp.zeros_like(a))
    hi = jnp.where(lane2 >= DN_DK, a, jnp.zeros_like(a))
    return jnp.concatenate([lo, hi], axis=0)


def _pair_blockdiag(a, bdmask):
    return jnp.where(bdmask, jnp.concatenate([a, a], axis=0), 0.0)


def _dn_batch_per_step(b):
    return DN_BATCH_PER_STEP if b % DN_BATCH_PER_STEP == 0 else 1


def _dn_chunk_body(pd_ref, prev_ref, next_ref, pab_ref, cw_ref, alog_ref, dtb_ref,
                   u_ref, w_ref, qk_ref, qd_ref, kd_ref, el_ref,
                   qkv_s, g_s, beta_s, *, n_ctx_tiles, n_tiles):
    nb, tm = pd_ref.shape[0], pd_ref.shape[1]
    c = DN_CHUNK
    j = pl.program_id(1)
    has_prev = jnp.logical_and(j != 0, j != n_ctx_tiles)
    has_next = jnp.logical_and(j != n_ctx_tiles - 1, j != n_tiles - 1)
    row = _iota((tm, 1), 0)
    ones_bd = _block_ones(BR_WIDTH, DN_DK).astype(BF16)
    for bb in range(nb):
        x = pd_ref[bb]
        prev_row = jnp.where(has_prev, prev_ref[bb, SUBLANES - 1:SUBLANES, :], 0.0)
        next_row = jnp.where(has_next, next_ref[bb, 0:1, :], 0.0)
        x_m = jnp.where(row == 0, prev_row, pltpu.roll(x, 1, axis=0))
        x_p = jnp.where(row == tm - 1, next_row, pltpu.roll(x, tm - 1, axis=0))
        y = x_m * cw_ref[0:1, :] + x * cw_ref[1:2, :] + x_p * cw_ref[2:3, :]
        y = y * _sigmoid(y)
        q = y[:, 0:BR_WIDTH]
        k = y[:, BR_WIDTH:2 * BR_WIDTH]
        q = q * lax.rsqrt(_sum_dot(q * q, ones_bd, 2) + LN_EPS) * (DN_DK ** -0.5)
        k = k * lax.rsqrt(_sum_dot(k * k, ones_bd, 2) + LN_EPS)
        qkv_s[bb, :, 0:BR_WIDTH] = q
        qkv_s[bb, :, BR_WIDTH:2 * BR_WIDTH] = k
        qkv_s[bb, :, 2 * BR_WIDTH:3 * BR_WIDTH] = y[:, 2 * BR_WIDTH:3 * BR_WIDTH]
        ab = pab_ref[bb]
        z = ab + dtb_ref[...]
        softplus = jnp.maximum(z, 0.0) + jnp.log(1.0 + jnp.exp(-jnp.abs(z)))
        g_s[bb] = -jnp.exp(alog_ref[...]) * softplus
        beta_s[bb] = _sigmoid(ab)

    ii = _iota((c, LANES), 0)
    lane2 = _iota((c, LANES), 1)
    jj = lane2 % c
    eye2 = (ii == jj).astype(F32)
    bdmask = _iota((LANES, LANES), 0) // c == _iota((LANES, LANES), 1) // c
    ri = _iota((c, c), 0)
    ci = _iota((c, c), 1)
    tri_f = (ri >= ci).astype(F32).astype(BF16)
    tri_b = (ri <= ci).astype(F32).astype(BF16)
    first_half = lane2 < DN_DK
    nh = DN_HEADS

    def chunk_group(grp, carry):
        systems = []
        for bb, cc in [(bb, cc) for bb in range(nb) for cc in range(DN_CHUNKS_PER_GROUP)]:
            ch = grp * DN_CHUNKS_PER_GROUP + cc
            rows = pl.ds(pl.multiple_of(ch * c, c), c)
            el_rows = pl.ds(pl.multiple_of(ch * SUBLANES, SUBLANES), SUBLANES)
            gc = g_s[bb, rows, :]
            bc = beta_s[bb, rows, :]
            gam = jnp.where(lane2 < nh, _sel_dot(tri_f, gc), _sel_dot(tri_b, gc))
            gam_t = jnp.concatenate([gam, gam], axis=0).T
            for d in range(2):
                incl = (ii >= jj) if d == 0 else (ii <= jj)
                strict = (ii > jj) if d == 0 else (ii < jj)
                for p in range(nh // 2):
                    l0 = d * nh + 2 * p
                    l1 = l0 + 1
                    gcol = jnp.where(first_half, gam[:, l0:l0 + 1], gam[:, l1:l1 + 1])
                    grow = jnp.where(first_half, gam_t[l0:l0 + 1, :], gam_t[l1:l1 + 1, :])
                    bcol = jnp.where(first_half, bc[:, 2 * nh + l0:2 * nh + l0 + 1],
                                     bc[:, 2 * nh + l1:2 * nh + l1 + 1])
                    kg = qkv_s[bb, rows, BR_WIDTH + p * LANES:BR_WIDTH + (p + 1) * LANES]
                    systems.append(dict(
                        bb=bb, d=d, rows=rows, el_rows=el_rows, lanes=slice(p * LANES, (p + 1) * LANES),
                        strict=strict, gcol=gcol, bcol=bcol, kg=kg, kb=kg * bcol,
                        dec=jnp.where(incl, jnp.exp(jnp.where(incl, gcol - grow, 0.0)), 0.0),
                        qg=qkv_s[bb, rows, p * LANES:(p + 1) * LANES],
                        vg=qkv_s[bb, rows, 2 * BR_WIDTH + p * LANES:2 * BR_WIDTH + (p + 1) * LANES],
                        kstack=_pair_stack(kg, lane2).astype(BF16)))
        for sy in systems:
            both = _dot_nt(jnp.concatenate([sy["kb"], sy["qg"]], axis=0).astype(BF16), sy["kstack"])
            sy["a"] = jnp.where(sy["strict"], both[0:c] * sy["dec"], 0.0)
            qk_ref[sy["bb"], sy["d"], sy["rows"], sy["lanes"]] = (both[c:2 * c] * sy["dec"]).astype(BF16)
        bd = lambda a: _pair_blockdiag(a, bdmask).astype(BF16)
        for sy in systems:
            pm = -jnp.where(ii // DN_INV_BLOCK == jj // DN_INV_BLOCK, sy["a"], 0.0)
            sy["t"] = eye2 + pm
            sy["pm"] = _dot(pm.astype(BF16), bd(pm))
        for sy in systems:
            both = _dot(jnp.concatenate([sy["t"], sy["pm"]], axis=0).astype(BF16), bd(sy["pm"]))
            sy["t"] = sy["t"] + both[0:c]
            sy["pm"] = both[c:2 * c]
        for sy in systems:
            sy["t"] = sy["t"] + _dot(sy["t"].astype(BF16), bd(sy["pm"]))
        size = DN_INV_BLOCK
        while size < c:
            coupling = (ii // (2 * size) == jj // (2 * size)) & (ii // size != jj // size)
            for sy in systems:
                sy["x"] = _dot(jnp.where(coupling, sy["a"], 0.0).astype(BF16), bd(sy["t"]))
            for sy in systems:
                sy["t"] = sy["t"] - _dot(sy["t"].astype(BF16), bd(sy["x"]))
            size *= 2
        for sy in systems:
            sy["eg"] = jnp.exp(sy["gcol"])
            rhs = jnp.concatenate([_pair_stack(sy["vg"] * sy["bcol"], lane2),
                                   _pair_stack(sy["kb"] * sy["eg"], lane2)], axis=1).astype(BF16)
            uw = _dot(sy["t"].astype(BF16), rhs)
            u_ref[sy["bb"], sy["d"], sy["rows"], sy["lanes"]] = uw[:, 0:LANES]
            w_ref[sy["bb"], sy["d"], sy["rows"], sy["lanes"]] = uw[:, LANES:2 * LANES].astype(BF16)
        for sy in systems:
            gcol = sy["gcol"]
            glast = gcol[c - 1:c, :] if sy["d"] == 0 else gcol[0:1, :]
            qd_ref[sy["bb"], sy["d"], sy["rows"], sy["lanes"]] = (sy["qg"] * sy["eg"]).astype(BF16)
            kd_ref[sy["bb"], sy["d"], sy["rows"], sy["lanes"]] = (sy["kg"] * jnp.exp(glast - gcol)).astype(BF16)
            el_ref[sy["bb"], sy["d"], sy["el_rows"], sy["lanes"]] = jnp.broadcast_to(jnp.exp(glast), (SUBLANES, LANES))
        return carry

    lax.fori_loop(0, tm // (c * DN_CHUNKS_PER_GROUP), chunk_group, 0)


def _dn_chunk(pd, pab, conv_w, alog, dtb, n_ctx):
    b, s, _ = pd.shape
    tm = TOKEN_TILE
    nt = s // tm
    nsub = tm // SUBLANES
    last_blk = s // SUBLANES - 1
    full = lambda a: pl.BlockSpec(a.shape, lambda i, j: (0,) * a.ndim)
    nb = _dn_batch_per_step(b)
    dir_tok = lambda: pl.BlockSpec((nb, 2, tm, BR_WIDTH), lambda i, j: (i, 0, j, 0))
    dir_shape = lambda dt: jax.ShapeDtypeStruct((b, 2, s, BR_WIDTH), dt)
    el_rows = (tm // DN_CHUNK) * SUBLANES
    return pl.pallas_call(
        functools.partial(_dn_chunk_body, n_ctx_tiles=n_ctx // tm, n_tiles=nt),
        grid=(b // nb, nt),
        in_specs=[pl.BlockSpec((nb, tm, PD_COLS), lambda i, j: (i, j, 0)),
                  pl.BlockSpec((nb, SUBLANES, PD_COLS), lambda i, j: (i, jnp.maximum(j * nsub - 1, 0), 0)),
                  pl.BlockSpec((nb, SUBLANES, PD_COLS),
                               lambda i, j: (i, jnp.minimum((j + 1) * nsub, last_blk), 0)),
                  pl.BlockSpec((nb, tm, PAB_COLS), lambda i, j: (i, j, 0)),
                  full(conv_w), full(alog), full(dtb)],
        out_specs=[dir_tok(), dir_tok(), dir_tok(), dir_tok(), dir_tok(),
                   pl.BlockSpec((nb, 2, el_rows, BR_WIDTH), lambda i, j: (i, 0, j, 0))],
        out_shape=[dir_shape(F32)] + [dir_shape(BF16)] * 4
        + [jax.ShapeDtypeStruct((b, 2, nt * el_rows, BR_WIDTH), F32)],
        scratch_shapes=[pltpu.VMEM((nb, tm, PD_COLS), F32), pltpu.VMEM((nb, tm, PAB_COLS), F32),
                        pltpu.VMEM((nb, tm, PAB_COLS), F32)],
        compiler_params=_cparams(("arbitrary", "arbitrary")),
        name="dn_chunk",
    )(pd, pd, pd, pab, conv_w, alog, dtb)


def _dn_scan_body(uf, wf, qkf, qdf, kdf, elf, ub, wb, qkb, qdb, kdb, elb, of_ref, ob_ref, s_ref):
    nb, tm = uf.shape[0], uf.shape[2]
    c = DN_CHUNK
    n_chunks = tm // c

    @pl.when(pl.program_id(1) == 0)
    def _():
        s_ref[...] = jnp.zeros_like(s_ref)

    bdmask = _iota((LANES, LANES), 0) // c == _iota((LANES, LANES), 1) // c
    views = ((uf, wf, qkf, qdf, kdf, elf, of_ref), (ub, wb, qkb, qdb, kdb, elb, ob_ref))
    chains = [(bb, d, p) for bb in range(nb) for d in range(2) for p in range(DN_HEADS // 2)]
    state = {ch: s_ref[ch] for ch in chains}
    for step in range(n_chunks):
        blk = {}
        for bb, d, p in chains:
            ck = step if d == 0 else n_chunks - 1 - step
            blk[bb, d, p] = (slice(ck * c, (ck + 1) * c), slice(p * LANES, (p + 1) * LANES), ck)
        tile = lambda k, ch: views[ch[1]][k][ch[0], 0, blk[ch][0], blk[ch][1]]
        sb = {ch: state[ch].astype(BF16) for ch in chains}
        ws = {ch: _dot(tile(1, ch), sb[ch]) for ch in chains}
        qs = {ch: _dot(tile(3, ch), sb[ch]) for ch in chains}
        vb = {ch: (tile(0, ch) - ws[ch]).astype(BF16) for ch in chains}
        for ch in chains:
            vbd = jnp.where(bdmask, jnp.concatenate([vb[ch], vb[ch]], axis=0), jnp.zeros((), BF16))
            views[ch[1]][6][ch[0], blk[ch][0], blk[ch][1]] = qs[ch] + _dot(tile(2, ch), vbd)
        upd = {ch: _dot_tn(tile(4, ch), vb[ch]) for ch in chains}
        for ch in chains:
            ck = blk[ch][2]
            el = views[ch[1]][5][ch[0], 0, ck * SUBLANES:ck * SUBLANES + 1, blk[ch][1]]
            state[ch] = state[ch] * el + jnp.where(bdmask, upd[ch], 0.0)
    for ch in chains:
        s_ref[ch] = state[ch]


def _dn_scan(u, w, qk, qd, kd, el, n_ctx):
    b, _, s, _ = u.shape
    tm = TOKEN_TILE
    nb = DN_SCAN_BATCH_PER_STEP if b % DN_SCAN_BATCH_PER_STEP == 0 else _dn_batch_per_step(b)
    nt = s // tm
    nctx = n_ctx // tm
    el_rows = (tm // DN_CHUNK) * SUBLANES

    def rev(j):
        return jnp.where(j < nctx, nctx - 1 - j, nt - 1 - (j - nctx))

    fwd = lambda rows: pl.BlockSpec((nb, 1, rows, BR_WIDTH), lambda i, j: (i, 0, j, 0))
    bwd = lambda rows: pl.BlockSpec((nb, 1, rows, BR_WIDTH), lambda i, j: (i, 1, rev(j), 0))
    in_specs = [fwd(tm)] * 5 + [fwd(el_rows)] + [bwd(tm)] * 5 + [bwd(el_rows)]
    out_shape = jax.ShapeDtypeStruct((b, s, BR_WIDTH), F32)
    return pl.pallas_call(
        _dn_scan_body,
        grid=(b // nb, nt),
        in_specs=in_specs,
        out_specs=[pl.BlockSpec((nb, tm, BR_WIDTH), lambda i, j: (i, j, 0)),
                   pl.BlockSpec((nb, tm, BR_WIDTH), lambda i, j: (i, rev(j), 0))],
        out_shape=[out_shape, out_shape],
        scratch_shapes=[pltpu.VMEM((nb, 2, DN_HEADS // 2, LANES, LANES), F32)],
        compiler_params=_cparams(("arbitrary", "arbitrary")),
        name="dn_scan",
    )(u, w, qk, qd, kd, el, u, w, qk, qd, kd, el)


def _merge_body(x_ref, mod_ref, h_ref, pg_ref, ya_ref, yb_ref, yc_ref, of_ref, ob_ref, dng_ref,
                wg_ref, wb_ref, wo_ref, lng_ref, lnb_ref, o_ref, *, alpha):
    hb = h_ref[0]
    o = of_ref[0] + ob_ref[0]
    ms = _sum_dot(o * o, _block_ones(BR_WIDTH, DN_DV).astype(BF16), 2) * (1.0 / DN_DV)
    yd = o * lax.rsqrt(ms + LN_EPS) * dng_ref[...]
    pg = pg_ref[0]
    sg = pg * _sigmoid(pg)
    acc = None
    for i, y in enumerate((ya_ref[0], yb_ref[0], yc_ref[0], yd)):
        t = (y * sg[:, i * BR_WIDTH:(i + 1) * BR_WIDTH]).astype(BF16)
        term = _sigmoid(_dot(hb, wg_ref[i])) * _dot(t, wb_ref[i])
        acc = term if acc is None else acc + term
    out = _dot(acc.astype(BF16), wo_ref[...])
    gt = mod_ref[0, 0, 2:3, :]
    r = alpha * x_ref[0] + gt * out
    mu = jnp.mean(r, -1, keepdims=True)
    rc = r - mu
    var = jnp.mean(rc * rc, -1, keepdims=True)
    o_ref[0] = rc * lax.rsqrt(var + LN_EPS) * lng_ref[...] + lnb_ref[...]


def _merge(xs, mod, h, pg, ya, yb, yc, o_f, o_b, dng, wg, wb, wo, lng, lnb, n_ctx_tiles, skip_tiles, alpha):
    b, s, d = xs.shape
    tm = TOKEN_TILE
    nt = s // tm - skip_tiles
    tok = lambda width: pl.BlockSpec((1, tm, width), lambda i, j: (i, j + skip_tiles, 0))
    full = lambda a: pl.BlockSpec(a.shape, lambda i, j: (0,) * a.ndim)
    return pl.pallas_call(
        functools.partial(_merge_body, alpha=alpha),
        grid=(b, nt),
        in_specs=[tok(d),
                  pl.BlockSpec((1, 1, 3, d),
                               lambda i, j: (i, jnp.where(j + skip_tiles >= n_ctx_tiles, 1, 0), 0, 0)),
                  tok(d), tok(PG_COLS), tok(BR_WIDTH), tok(BR_WIDTH), tok(BR_WIDTH), tok(BR_WIDTH),
                  tok(BR_WIDTH), full(dng), full(wg), full(wb), full(wo), full(lng), full(lnb)],
        out_specs=pl.BlockSpec((1, tm, d), lambda i, j: (i, j, 0)),
        out_shape=jax.ShapeDtypeStruct((b, nt * tm, d), F32),
        compiler_params=_cparams(("arbitrary", "arbitrary")),
        name="merge",
    )(xs, mod, h, pg, ya, yb, yc, o_f, o_b, dng, wg, wb, wo, lng, lnb)


def _rope_swap_index(n_blocks):
    blk = jnp.array(list(range(8, 16)) + list(range(0, 8)) + list(range(24, 32)) + list(range(16, 24)))
    return (jnp.arange(n_blocks)[:, None] * 32 + blk[None, :]).reshape(-1)


def _pack_w_in(w):
    d = w.shape[0]
    o = 0
    cq = w[:, o:o + MLA_Q_LORA]; o += MLA_Q_LORA
    ckv = w[:, o:o + MLA_KV_LORA]; o += MLA_KV_LORA
    kr = w[:, o:o + MLA_ROPE]; o += MLA_ROPE
    pb = w[:, o:o + GMLP_COLS]; o += GMLP_COLS
    dq = w[:, o:o + 256]; dk = w[:, o + 256:o + 512]; dv = w[:, o + 512:o + 768]; o += DIFF_COLS
    dn_qkv = w[:, o:o + 3 * BR_WIDTH]; dn_ab = w[:, o + 3 * BR_WIDTH:o + DN_COLS]; o += DN_COLS
    pg = w[:, o:]
    z = lambda n: jnp.zeros((d, n), w.dtype)
    sw1 = _rope_swap_index(1)
    sw8 = _rope_swap_index(8)
    place = lambda a: jnp.concatenate([z(MLA_NOPE), a, z(LANES - MLA_NOPE - MLA_ROPE)], axis=1)
    packed = jnp.concatenate(
        [cq, ckv, place(kr), place(kr[:, sw1]), pb, dq, dk, dv, dq[:, sw8], dk[:, sw8], dn_qkv,
         dn_ab, z(PAB_COLS - 4 * DN_HEADS), pg], axis=1)
    return packed.astype(BF16)


def _pack_mla_weights(w_uq, w_ukv):
    dq = MLA_NOPE + MLA_ROPE
    wq = w_uq.reshape(MLA_Q_LORA, MLA_HEADS, dq)
    zq = jnp.zeros((MLA_Q_LORA, MLA_HEADS, LANES - dq), w_uq.dtype)
    wq_p = jnp.concatenate([wq, zq], axis=2).reshape(MLA_Q_LORA, MLA_HEADS * LANES)
    rope_sw = wq[:, :, MLA_NOPE:][:, :, _rope_swap_index(1)]
    wqs_p = jnp.concatenate([jnp.zeros((MLA_Q_LORA, MLA_HEADS, MLA_NOPE), w_uq.dtype), rope_sw, zq],
                            axis=2).reshape(MLA_Q_LORA, MLA_HEADS * LANES)
    wkv = w_ukv.reshape(MLA_KV_LORA, MLA_HEADS, MLA_NOPE + MLA_V)
    zk = jnp.zeros((MLA_KV_LORA, MLA_HEADS, LANES - MLA_NOPE), w_ukv.dtype)
    wk_p = jnp.concatenate([wkv[:, :, :MLA_NOPE], zk], axis=2).reshape(MLA_KV_LORA, MLA_HEADS * LANES)
    wv_p = wkv[:, :, MLA_NOPE:].reshape(MLA_KV_LORA, MLA_HEADS * MLA_V)
    return wq_p.astype(BF16), wqs_p.astype(BF16), wk_p.astype(BF16), wv_p.astype(BF16)


def _rope_tables(n, n_ctx):
    rows = n // GRID_W
    row = jnp.repeat(jnp.arange(rows, dtype=F32), GRID_W)
    col = jnp.tile(jnp.arange(GRID_W, dtype=F32), rows)
    axis_dim = MLA_ROPE // 2
    inv_freq = ROPE_BASE ** (-jnp.arange(0, axis_dim, 2, dtype=F32) / axis_dim)
    ar = row[:, None] * inv_freq
    ac = col[:, None] * inv_freq
    cos32 = jnp.concatenate([jnp.cos(ar), jnp.cos(ar), jnp.cos(ac), jnp.cos(ac)], axis=1)
    sin32 = jnp.concatenate([-jnp.sin(ar), jnp.sin(ar), -jnp.sin(ac), jnp.sin(ac)], axis=1)
    cos32 = jnp.concatenate([jnp.ones((n_ctx, 32), F32), cos32], axis=0)
    sin32 = jnp.concatenate([jnp.zeros((n_ctx, 32), F32), sin32], axis=0)
    s = n + n_ctx
    one = lambda w: jnp.ones((s, w), F32)
    zero = lambda w: jnp.zeros((s, w), F32)
    mla_cos = jnp.concatenate([one(MLA_NOPE), cos32, one(LANES - MLA_NOPE - MLA_ROPE)], axis=1)
    mla_sin = jnp.concatenate([zero(MLA_NOPE), sin32, zero(LANES - MLA_NOPE - MLA_ROPE)], axis=1)
    diff_cos = jnp.tile(cos32, (1, LANES // 32))
    diff_sin = jnp.tile(sin32, (1, LANES // 32))
    return mla_cos, mla_sin, diff_cos, diff_sin


def _pad_lanes(a, width=LANES):
    return jnp.concatenate([a, jnp.zeros(a.shape[:-1] + (width - a.shape[-1],), a.dtype)], axis=-1)


def kernel(x, c, ctx, c_ctx, w_mod, b_mod, w_in, mla_q_norm, mla_w_uq, mla_kv_norm, mla_w_ukv, gmlp_ln_g, gmlp_w_s, gmlp_b_s, diff_lq1, diff_lk1, diff_lq2, diff_lk2, diff_norm_g, dn_conv_w, dn_a_log, dn_dt_bias, dn_norm_g, w_gate, w_branch, w_out, ln_g, ln_b):
    b, n, d = x.shape
    n_ctx = ctx.shape[1]
    depth = w_mod.shape[0]
    tm = TOKEN_TILE
    assert d == D_MODEL and n % tm == 0 and n_ctx % tm == 0 and n % GRID_W == 0
    assert tm == ATTN_Q_TILE == ATTN_K_TILE
    alpha = (2 * depth) ** 0.25
    n_ctx_tiles = n_ctx // tm

    xs = jnp.concatenate([ctx, x], axis=1)
    mla_cos, mla_sin, diff_cos, diff_sin = _rope_tables(n, n_ctx)

    rows = ((b + 1 + SUBLANES - 1) // SUBLANES) * SUBLANES
    cc = jnp.concatenate([c, c_ctx[None, :], jnp.zeros((rows - b - 1, d), F32)], axis=0)
    mod_all = _modulation(cc, w_mod, b_mod)

    for l in range(depth):
        last = l == depth - 1
        lam_init = 0.8 - 0.6 * math.exp(-0.3 * l)
        mod_l = mod_all[l].reshape(rows, 3, d)
        mod = jnp.stack([jnp.broadcast_to(mod_l[b][None], (b, 3, d)), mod_l[:b]], axis=1)

        wq, wqs, wk, wv = _pack_mla_weights(mla_w_uq[l], mla_w_ukv[l])
        bias = jnp.repeat(gmlp_b_s[l].T, BR_WIDTH // GMLP_GROUPS, axis=1)
        h, pd, pab, pg, q_a, k_a, vt_a, yb, q_c, k_c, vt_c = _inproj(
            xs, mod, _pack_w_in(w_in[l]), (mla_cos, mla_sin, diff_cos, diff_sin),
            (mla_q_norm[l][None, :], mla_kv_norm[l][None, :], wq, wqs, wk, wv),
            (gmlp_ln_g[l][None, :], gmlp_w_s[l].astype(BF16), bias), n_ctx_tiles)
        ya = _mla_attn(q_a, k_a, vt_a, n_ctx)
        lqk = _pad_lanes(jnp.stack([diff_lq1[l], diff_lk1[l], diff_lq2[l], diff_lk2[l]], axis=0))
        yc = _diff_attn(q_c, k_c, vt_c, lqk, diff_norm_g[l][:, None], n_ctx, lam_init)

        conv_w = jnp.concatenate([dn_conv_w[l], jnp.zeros((SUBLANES - 3, PD_COLS), F32)], axis=0)
        alog = _pad_lanes(dn_a_log[l].reshape(1, 2 * DN_HEADS))
        dtb = _pad_lanes(dn_dt_bias[l].reshape(1, 2 * DN_HEADS))
        u, w, qk, qd, kd, el = _dn_chunk(pd, pab, conv_w, alog, dtb, n_ctx)
        o_f, o_b = _dn_scan(u, w, qk, qd, kd, el, n_ctx)

        dng = jnp.tile(dn_norm_g[l], DN_HEADS)[None, :]
        xs = _merge(xs, mod, h, pg, ya, yb, yc, o_f, o_b, dng, w_gate[l].astype(BF16),
                    w_branch[l].astype(BF16), w_out[l].astype(BF16), ln_g[l][None, :], ln_b[l][None, :],
                    n_ctx_tiles, n_ctx_tiles if last else 0, alpha)
    return xs
```

```python
import functools
import math

import jax
import jax.numpy as jnp
from jax import lax
from jax.experimental import pallas as pl
from jax.experimental.pallas import tpu as pltpu

F32 = jnp.float32
BF16 = jnp.bfloat16

D_MODEL = 1024
GRID_W = 64
N_BRANCH = 4
BR_WIDTH = 256
MLA_HEADS = 4
MLA_NOPE = 64
MLA_ROPE = 32
MLA_V = 64
MLA_Q_LORA = 256
MLA_KV_LORA = 128
GMLP_GROUPS = 4
GMLP_CHUNK = 128
DIFF_HEADS = 4
DIFF_D = 32
DN_HEADS = 4
DN_DK = 64
DN_DV = 64
DN_CHUNK = 64
ROPE_BASE = 10000.0
LN_EPS = 1e-6

MLA_COLS = MLA_Q_LORA + MLA_KV_LORA + MLA_ROPE
GMLP_COLS = 2 * BR_WIDTH
DIFF_COLS = 3 * DIFF_HEADS * 2 * DIFF_D
DN_COLS = 3 * BR_WIDTH + 4 * DN_HEADS

LANES = 128
SUBLANES = 8
VMEM_LIMIT_BYTES = 56 * 1024 * 1024

TOKEN_TILE = 256
ATTN_Q_TILE = 256
ATTN_K_TILE = 256
DN_CHUNKS_PER_GROUP = 4
DN_BATCH_PER_STEP = 2
DN_SCAN_BATCH_PER_STEP = 4
DN_INV_BLOCK = 8
ATTN_STEPS_PER_TRIP = 128
ATTN_LOOKAHEAD = 5
NEG_BIG = -1e30
VT_ROWS = 80
LOG2_E = 1.4426950408889634

PA_COLS = 640
PB_COLS = 512
PC_COLS = 1280
PD_COLS = 768
PAB_COLS = 128
PG_COLS = 1024
PACK_SPLITS = (PA_COLS, PB_COLS, PC_COLS, PD_COLS, PAB_COLS, PG_COLS)
PACK_COLS = sum(PACK_SPLITS)


def _cparams(semantics):
    return pltpu.CompilerParams(dimension_semantics=semantics, vmem_limit_bytes=VMEM_LIMIT_BYTES)


def _dot(a, b):
    return jnp.dot(a, b, preferred_element_type=F32)


def _dot_nt(a, b):
    return lax.dot_general(a, b, (((1,), (1,)), ((), ())), preferred_element_type=F32)


def _dot_tn(a, b):
    return lax.dot_general(a, b, (((0,), (0,)), ((), ())), preferred_element_type=F32)


def _split_bf16(x, terms):
    pieces = []
    for _ in range(terms):
        hi = x.astype(BF16)
        pieces.append(hi)
        x = x - hi.astype(F32)
    return pieces


def _sum_dot(x, sel, terms=3):
    return sum(_dot(p, sel) for p in _split_bf16(x, terms))


def _sel_dot(sel, x, terms=3):
    return sum(_dot(sel, p) for p in _split_bf16(x, terms))


def _sigmoid(x):
    return 1.0 / (1.0 + jnp.exp(-x))


def _iota(shape, axis):
    return lax.broadcasted_iota(jnp.int32, shape, axis)


def _block_ones(n, blk):
    return (_iota((n, n), 0) // blk == _iota((n, n), 1) // blk).astype(F32)


def _mod_body(c_ref, w_ref, b_ref, o_ref):
    c = c_ref[...]
    s = (c * _sigmoid(c)).astype(BF16)
    o_ref[0] = _dot(s, w_ref[0].astype(BF16)) + b_ref[0]


def _modulation(cc, w_mod, b_mod):
    depth, d, n3 = w_mod.shape
    rows = cc.shape[0]
    tn = 1024
    return pl.pallas_call(
        _mod_body,
        grid=(depth, n3 // tn),
        in_specs=[pl.BlockSpec((rows, d), lambda l, j: (0, 0)),
                  pl.BlockSpec((1, d, tn), lambda l, j: (l, 0, j)),
                  pl.BlockSpec((1, 1, tn), lambda l, j: (l, 0, j))],
        out_specs=pl.BlockSpec((1, rows, tn), lambda l, j: (l, 0, j)),
        out_shape=jax.ShapeDtypeStruct((depth, rows, n3), F32),
        compiler_params=_cparams(("arbitrary", "arbitrary")),
        name="modulation",
    )(cc, w_mod, b_mod.reshape(depth, 1, n3))


def _rms(x, g):
    return x * lax.rsqrt(jnp.mean(x * x, -1, keepdims=True) + LN_EPS) * g


def _mla_prep_values(pa, cos, sin, qn, kvn, wq, wqs, wk, wv, scale):
    cq = pa[:, 0:MLA_Q_LORA]
    ckv = pa[:, MLA_Q_LORA:MLA_Q_LORA + MLA_KV_LORA]
    kr = pa[:, 384:512]
    kr_sw = pa[:, 512:640]
    cos4 = jnp.concatenate([cos] * MLA_HEADS, axis=1)
    sin4 = jnp.concatenate([sin] * MLA_HEADS, axis=1)
    cqn = _rms(cq, qn).astype(BF16)
    q = ((_dot(cqn, wq) * cos4 + _dot(cqn, wqs) * sin4) * scale).astype(BF16)
    ckvn = _rms(ckv, kvn).astype(BF16)
    kr_rot = kr * cos + kr_sw * sin
    k = (_dot(ckvn, wk) + jnp.concatenate([kr_rot] * MLA_HEADS, axis=1)).astype(BF16)
    return q, k, _vt_with_ones(_dot(ckvn, wv), MLA_HEADS, MLA_V)


def _diff_prep_values(pc, cos, sin, scale):
    cos = jnp.concatenate([cos] * 2, axis=1)
    sin = jnp.concatenate([sin] * 2, axis=1)
    q = ((pc[:, 0:256] * cos + pc[:, 768:1024] * sin) * scale).astype(BF16)
    k = (pc[:, 256:512] * cos + pc[:, 1024:1280] * sin).astype(BF16)
    return q, k, _vt_with_ones(pc[:, 512:768], DIFF_HEADS, 2 * DIFF_D)


def _gelu_tanh(x):
    return 0.5 * x * (1.0 + jnp.tanh(math.sqrt(2.0 / math.pi) * (x + 0.044715 * (x * x * x))))


def _gmlp_values(pb, g, ws_ref, bias_ref, o_ref):
    z = _gelu_tanh(pb)
    u = z[:, 0:BR_WIDTH]
    v = z[:, BR_WIDTH:2 * BR_WIDTH]
    mu = jnp.mean(v, -1, keepdims=True)
    vc = v - mu
    var = jnp.mean(vc * vc, -1, keepdims=True)
    v = (vc * lax.rsqrt(var + LN_EPS) * g).astype(BF16)
    lane = _iota((GMLP_CHUNK, LANES), 1)
    gw = BR_WIDTH // GMLP_GROUPS
    for c in range(pb.shape[0] // GMLP_CHUNK):
        rows = slice(c * GMLP_CHUNK, (c + 1) * GMLP_CHUNK)
        for half in range(BR_WIDTH // LANES):
            cols = slice(half * LANES, (half + 1) * LANES)
            vch = v[rows, cols]
            a0 = _dot(ws_ref[2 * half], vch)
            a1 = _dot(ws_ref[2 * half + 1], vch)
            mixed = jnp.where(lane < gw, a0, a1) + bias_ref[:, cols]
            o_ref[0, rows, cols] = u[rows, cols] * mixed


def _inproj_body(ctx_ref, x_ref, mod_ref, w_ref, cosm_ref, sinm_ref, cosd_ref, sind_ref, qn_ref, kvn_ref,
                 wq_ref, wqs_ref, wk_ref, wv_ref, gg_ref, ws_ref, bias_ref,
                 h_ref, pd_ref, pab_ref, pg_ref, qa_ref, ka_ref, vta_ref, yb_ref, qc_ref, kc_ref, vtc_ref,
                 *, mla_scale, diff_scale, n_ctx_tiles):
    x = jnp.where(pl.program_id(1) < n_ctx_tiles, ctx_ref[0], x_ref[0])
    mu = jnp.mean(x, -1, keepdims=True)
    xc = x - mu
    var = jnp.mean(xc * xc, -1, keepdims=True)
    xn = xc * lax.rsqrt(var + LN_EPS)
    sh = mod_ref[0, 0, 0:1, :]
    sc = mod_ref[0, 0, 1:2, :]
    hb = (xn * (1.0 + sc) + sh).astype(BF16)
    h_ref[0] = hb
    offs = [sum(PACK_SPLITS[:i]) for i in range(len(PACK_SPLITS))]
    proj = lambda i: _dot(hb, w_ref[:, offs[i]:offs[i] + PACK_SPLITS[i]])
    pa = proj(0)
    pb = proj(1)
    pc = proj(2)
    qa_ref[0], ka_ref[0], vta_ref[0, 0] = _mla_prep_values(
        pa, cosm_ref[...], sinm_ref[...], qn_ref[...], kvn_ref[...], wq_ref[...], wqs_ref[...],
        wk_ref[...], wv_ref[...], mla_scale)
    pd_ref[0] = proj(3)
    _gmlp_values(pb, gg_ref[...], ws_ref, bias_ref, yb_ref)
    pg_ref[0] = proj(5)
    qc_ref[0], kc_ref[0], vtc_ref[0, 0] = _diff_prep_values(pc, cosd_ref[...], sind_ref[...], diff_scale)
    pab_ref[0] = proj(4)


def _stream_specs(tm, d, n_ctx_tiles, skip_tiles=0):
    return [pl.BlockSpec((1, tm, d), lambda i, j: (i, jnp.minimum(j + skip_tiles, n_ctx_tiles - 1), 0)),
            pl.BlockSpec((1, tm, d), lambda i, j: (i, jnp.maximum(j + skip_tiles - n_ctx_tiles, 0), 0))]


def _inproj(ctx_s, x_s, mod, w_pack, tables, mla_params, gmlp_params):
    b, n_ctx, d = ctx_s.shape
    tm = TOKEN_TILE
    n_ctx_tiles = n_ctx // tm
    s = n_ctx + x_s.shape[1]
    hp = MLA_HEADS * LANES
    tok = lambda width, dt: jax.ShapeDtypeStruct((b, s, width), dt)
    tok_spec = lambda width: pl.BlockSpec((1, tm, width), lambda i, j: (i, j, 0))
    full = lambda a: pl.BlockSpec(a.shape, lambda i, j: (0,) * a.ndim)
    table = pl.BlockSpec((tm, LANES), lambda i, j: (j, 0))
    vt_spec = pl.BlockSpec((1, 1, 4 * VT_ROWS, tm), lambda i, j: (i, j, 0, 0))
    vt_shape = jax.ShapeDtypeStruct((b, s // tm, 4 * VT_ROWS, tm), BF16)
    consts = list(mla_params) + list(gmlp_params)
    return pl.pallas_call(
        functools.partial(_inproj_body, mla_scale=(MLA_NOPE + MLA_ROPE) ** -0.5 * LOG2_E,
                          diff_scale=DIFF_D ** -0.5 * LOG2_E, n_ctx_tiles=n_ctx_tiles),
        grid=(b, s // tm),
        in_specs=_stream_specs(tm, d, n_ctx_tiles) + [
                  pl.BlockSpec((1, 1, 3, d), lambda i, j: (i, jnp.where(j >= n_ctx_tiles, 1, 0), 0, 0)),
                  pl.BlockSpec((d, PACK_COLS), lambda i, j: (0, 0))]
        + [table] * 4 + [full(a) for a in consts],
        out_specs=[tok_spec(d), tok_spec(PD_COLS), tok_spec(PAB_COLS), tok_spec(PG_COLS),
                   tok_spec(hp), tok_spec(hp), vt_spec, tok_spec(BR_WIDTH),
                   tok_spec(BR_WIDTH), tok_spec(BR_WIDTH), vt_spec],
        out_shape=[tok(d, BF16), tok(PD_COLS, F32), tok(PAB_COLS, F32), tok(PG_COLS, F32),
                   tok(hp, BF16), tok(hp, BF16), vt_shape, tok(BR_WIDTH, F32),
                   tok(BR_WIDTH, BF16), tok(BR_WIDTH, BF16), vt_shape],
        compiler_params=_cparams(("arbitrary", "arbitrary")),
        name="inproj",
    )(ctx_s, x_s, mod, w_pack, *tables, *consts)


def _vt_with_ones(v, heads, dv):
    tm = v.shape[0]
    vt = v.T
    aug = (_iota((VT_ROWS - dv, tm), 0) == 0).astype(F32)
    pieces = []
    for h in range(heads):
        pieces += [vt[h * dv:(h + 1) * dv], aug]
    return jnp.concatenate(pieces, axis=0).astype(BF16)


def _tiles_per_iter(n_tiles, n_chain):
    t = max(1, ATTN_STEPS_PER_TRIP // n_chain)
    while n_tiles % t:
        t -= 1
    return t


def _flash_chains(qt_s, k_ref, vt_ref, k_lanes, v_rows, n_k, n_first, tiles_per_iter, m_s, acc_s):
    tk = ATTN_K_TILE
    m_s[...] = jnp.full(m_s.shape, NEG_BIG, F32)
    acc_s[...] = jnp.zeros(acc_s.shape, F32)

    n_chain = len(k_lanes)

    def scores(c, kt):
        k_rows = pl.ds(pl.multiple_of(kt * tk, tk), tk)
        return _dot(k_ref[0, k_rows, k_lanes[c][0]:k_lanes[c][1]], qt_s[c])

    def run(kt0, n_iter, tiles_per_iter):
        per_iter = tiles_per_iter * n_chain

        def step(it, ahead):
            ahead = list(ahead)
            base = kt0 + it * tiles_per_iter
            for idx in range(per_iter):
                kt, c = base + idx // n_chain, idx % n_chain
                s = ahead.pop(0)
                nxt = idx + ATTN_LOOKAHEAD
                ahead.append(scores(nxt % n_chain, jnp.minimum(base + nxt // n_chain, n_k - 1)))
                m_old = m_s[c]
                m_new = jnp.maximum(m_old, jnp.max(s, axis=0, keepdims=True))
                p = jnp.exp2(s - m_new).astype(BF16)
                vt = vt_ref[0, kt, v_rows[c][0]:v_rows[c][1], :]
                acc_s[c] = jnp.exp2(m_old - m_new) * acc_s[c] + _dot(vt, p)
                m_s[c] = m_new
            return tuple(ahead)

        init = tuple(scores(i % n_chain, jnp.minimum(kt0 + i // n_chain, n_k - 1))
                     for i in range(ATTN_LOOKAHEAD))
        lax.fori_loop(0, n_iter, step, init)

    run(0, n_first, 1)
    run(n_first, (n_k - n_first) // tiles_per_iter, tiles_per_iter)


def _mla_attn_body(q_ref, k_ref, vt_ref, o_ref, qt_s, m_s, acc_s, *, n_ctx_q_tiles, n_ctx_k_tiles,
                   n_k_tiles):
    j = pl.program_id(1)
    n_k = jnp.where(j < n_ctx_q_tiles, n_ctx_k_tiles, n_k_tiles)
    k_lanes = [(h * LANES, (h + 1) * LANES) for h in range(MLA_HEADS)]
    v_rows = [(h * VT_ROWS, (h + 1) * VT_ROWS) for h in range(MLA_HEADS)]
    for h in range(MLA_HEADS):
        qt_s[h] = q_ref[0, :, h * LANES:(h + 1) * LANES].astype(F32).T.astype(BF16)
    _flash_chains(qt_s, k_ref, vt_ref, k_lanes, v_rows, n_k, n_ctx_k_tiles,
                  _tiles_per_iter(n_k_tiles - n_ctx_k_tiles, MLA_HEADS), m_s, acc_s)
    outs = []
    for h in range(MLA_HEADS):
        acc = acc_s[h]
        outs.append(acc[0:MLA_V] / acc[MLA_V:MLA_V + 1])
    o_ref[0] = jnp.concatenate(outs, axis=0).T


def _mla_attn(q, k, vt, n_ctx):
    b, s, hp = q.shape
    tq, tk = ATTN_Q_TILE, ATTN_K_TILE
    return pl.pallas_call(
        functools.partial(_mla_attn_body, n_ctx_q_tiles=n_ctx // tq, n_ctx_k_tiles=n_ctx // tk,
                          n_k_tiles=s // tk),
        grid=(b, s // tq),
        in_specs=[pl.BlockSpec((1, tq, hp), lambda i, j: (i, j, 0)),
                  pl.BlockSpec((1, s, hp), lambda i, j: (i, 0, 0)),
                  pl.BlockSpec((1, s // tk, MLA_HEADS * VT_ROWS, tk), lambda i, j: (i, 0, 0, 0))],
        out_specs=pl.BlockSpec((1, tq, BR_WIDTH), lambda i, j: (i, j, 0)),
        out_shape=jax.ShapeDtypeStruct((b, s, BR_WIDTH), F32),
        scratch_shapes=[pltpu.VMEM((MLA_HEADS, LANES, tq), BF16), pltpu.VMEM((MLA_HEADS, 1, tq), F32),
                        pltpu.VMEM((MLA_HEADS, VT_ROWS, tq), F32)],
        compiler_params=_cparams(("arbitrary", "arbitrary")),
        name="mla_attn",
    )(q, k, vt)


def _diff_attn_body(q_ref, k_ref, vt_ref, lqk_ref, g_ref, o_ref, qm_s, m_s, acc_s, *, n_ctx_q_tiles,
                    n_ctx_k_tiles, n_k_tiles, lam_init):
    tq = q_ref.shape[1]
    j = pl.program_id(1)
    n_k = jnp.where(j < n_ctx_q_tiles, n_ctx_k_tiles, n_k_tiles)
    lqk = lqk_ref[...]
    lam = (jnp.exp(jnp.sum(lqk[0:1] * lqk[1:2], axis=1, keepdims=True))
           - jnp.exp(jnp.sum(lqk[2:3] * lqk[3:4], axis=1, keepdims=True)) + lam_init)
    row = _iota((LANES, tq), 0)
    dv = 2 * DIFF_D
    k_lanes, v_rows = [], []
    for grp in range(2 * DIFF_HEADS * DIFF_D // LANES):
        qgt = q_ref[0, :, grp * LANES:(grp + 1) * LANES].astype(F32).T
        for sub in range(LANES // DIFF_D):
            blk = grp * (LANES // DIFF_D) + sub
            keep = (row >= sub * DIFF_D) & (row < (sub + 1) * DIFF_D)
            qm_s[blk] = jnp.where(keep, qgt, 0.0).astype(BF16)
            k_lanes.append((grp * LANES, (grp + 1) * LANES))
            v_rows.append(((blk // 2) * VT_ROWS, (blk // 2 + 1) * VT_ROWS))
    _flash_chains(qm_s, k_ref, vt_ref, k_lanes, v_rows, n_k, n_ctx_k_tiles,
                  _tiles_per_iter(n_k_tiles - n_ctx_k_tiles, 2 * DIFF_HEADS), m_s, acc_s)
    outs = []
    for h in range(DIFF_HEADS):
        a1 = acc_s[2 * h]
        a2 = acc_s[2 * h + 1]
        o = a1[0:dv] / a1[dv:dv + 1] - lam * (a2[0:dv] / a2[dv:dv + 1])
        o = o * lax.rsqrt(jnp.mean(o * o, axis=0, keepdims=True) + LN_EPS)
        outs.append(o * g_ref[...] * (1.0 - lam_init))
    o_ref[0] = jnp.concatenate(outs, axis=0).T


def _diff_attn(q, k, vt, lqk, g_col, n_ctx, lam_init):
    b, s, _ = q.shape
    tq, tk = ATTN_Q_TILE, ATTN_K_TILE
    return pl.pallas_call(
        functools.partial(_diff_attn_body, n_ctx_q_tiles=n_ctx // tq, n_ctx_k_tiles=n_ctx // tk,
                          n_k_tiles=s // tk, lam_init=lam_init),
        grid=(b, s // tq),
        in_specs=[pl.BlockSpec((1, tq, BR_WIDTH), lambda i, j: (i, j, 0)),
                  pl.BlockSpec((1, s, BR_WIDTH), lambda i, j: (i, 0, 0)),
                  pl.BlockSpec((1, s // tk, DIFF_HEADS * VT_ROWS, tk), lambda i, j: (i, 0, 0, 0)),
                  pl.BlockSpec(lqk.shape, lambda i, j: (0, 0)),
                  pl.BlockSpec(g_col.shape, lambda i, j: (0, 0))],
        out_specs=pl.BlockSpec((1, tq, BR_WIDTH), lambda i, j: (i, j, 0)),
        out_shape=jax.ShapeDtypeStruct((b, s, BR_WIDTH), F32),
        scratch_shapes=[pltpu.VMEM((2 * DIFF_HEADS, LANES, tq), BF16),
                        pltpu.VMEM((2 * DIFF_HEADS, 1, tq), F32),
                        pltpu.VMEM((2 * DIFF_HEADS, VT_ROWS, tq), F32)],
        compiler_params=_cparams(("arbitrary", "arbitrary")),
        name="diff_attn",
    )(q, k, vt, lqk, g_col)


def _pair_stack(a, lane2):
    lo = jnp.where(lane2 < DN_DK, a, jnp.zeros_like(a))
    hi = jnp.where(lane2 >= DN_DK, a, jnp.zeros_like(a))
    return jnp.concatenate([lo, hi], axis=0)


def _pair_blockdiag(a, bdmask):
    return jnp.where(bdmask, jnp.concatenate([a, a], axis=0), 0.0)


def _dn_batch_per_step(b):
    return DN_BATCH_PER_STEP if b % DN_BATCH_PER_STEP == 0 else 1


def _dn_chunk_body(pd_ref, prev_ref, next_ref, pab_ref, cw_ref, alog_ref, dtb_ref,
                   u_ref, w_ref, qk_ref, qd_ref, kd_ref, el_ref,
                   qkv_s, g_s, beta_s, *, n_ctx_tiles, n_tiles):
    nb, tm = pd_ref.shape[0], pd_ref.shape[1]
    c = DN_CHUNK
    j = pl.program_id(1)
    has_prev = jnp.logical_and(j != 0, j != n_ctx_tiles)
    has_next = jnp.logical_and(j != n_ctx_tiles - 1, j != n_tiles - 1)
    row = _iota((tm, 1), 0)
    ones_bd = _block_ones(BR_WIDTH, DN_DK).astype(BF16)
    for bb in range(nb):
        x = pd_ref[bb]
        prev_row = jnp.where(has_prev, prev_ref[bb, SUBLANES - 1:SUBLANES, :], 0.0)
        next_row = jnp.where(has_next, next_ref[bb, 0:1, :], 0.0)
        x_m = jnp.where(row == 0, prev_row, pltpu.roll(x, 1, axis=0))
        x_p = jnp.where(row == tm - 1, next_row, pltpu.roll(x, tm - 1, axis=0))
        y = x_m * cw_ref[0:1, :] + x * cw_ref[1:2, :] + x_p * cw_ref[2:3, :]
        y = y * _sigmoid(y)
        q = y[:, 0:BR_WIDTH]
        k = y[:, BR_WIDTH:2 * BR_WIDTH]
        q = q * lax.rsqrt(_sum_dot(q * q, ones_bd, 2) + LN_EPS) * (DN_DK ** -0.5)
        k = k * lax.rsqrt(_sum_dot(k * k, ones_bd, 2) + LN_EPS)
        qkv_s[bb, :, 0:BR_WIDTH] = q
        qkv_s[bb, :, BR_WIDTH:2 * BR_WIDTH] = k
        qkv_s[bb, :, 2 * BR_WIDTH:3 * BR_WIDTH] = y[:, 2 * BR_WIDTH:3 * BR_WIDTH]
        ab = pab_ref[bb]
        z = ab + dtb_ref[...]
        softplus = jnp.maximum(z, 0.0) + jnp.log(1.0 + jnp.exp(-jnp.abs(z)))
        g_s[bb] = -jnp.exp(alog_ref[...]) * softplus
        beta_s[bb] = _sigmoid(ab)

    ii = _iota((c, LANES), 0)
    lane2 = _iota((c, LANES), 1)
    jj = lane2 % c
    eye2 = (ii == jj).astype(F32)
    bdmask = _iota((LANES, LANES), 0) // c == _iota((LANES, LANES), 1) // c
    ri = _iota((c, c), 0)
    ci = _iota((c, c), 1)
    tri_f = (ri >= ci).astype(F32).astype(BF16)
    tri_b = (ri <= ci).astype(F32).astype(BF16)
    first_half = lane2 < DN_DK
    nh = DN_HEADS

    def chunk_group(grp, carry):
        systems = []
        for bb, cc in [(bb, cc) for bb in range(nb) for cc in range(DN_CHUNKS_PER_GROUP)]:
            ch = grp * DN_CHUNKS_PER_GROUP + cc
            rows = pl.ds(pl.multiple_of(ch * c, c), c)
            el_rows = pl.ds(pl.multiple_of(ch * SUBLANES, SUBLANES), SUBLANES)
            gc = g_s[bb, rows, :]
            bc = beta_s[bb, rows, :]
            gam = jnp.where(lane2 < nh, _sel_dot(tri_f, gc), _sel_dot(tri_b, gc))
            gam_t = jnp.concatenate([gam, gam], axis=0).T
            for d in range(2):
                incl = (ii >= jj) if d == 0 else (ii <= jj)
                strict = (ii > jj) if d == 0 else (ii < jj)
                for p in range(nh // 2):
                    l0 = d * nh + 2 * p
                    l1 = l0 + 1
                    gcol = jnp.where(first_half, gam[:, l0:l0 + 1], gam[:, l1:l1 + 1])
                    grow = jnp.where(first_half, gam_t[l0:l0 + 1, :], gam_t[l1:l1 + 1, :])
                    bcol = jnp.where(first_half, bc[:, 2 * nh + l0:2 * nh + l0 + 1],
                                     bc[:, 2 * nh + l1:2 * nh + l1 + 1])
                    kg = qkv_s[bb, rows, BR_WIDTH + p * LANES:BR_WIDTH + (p + 1) * LANES]
                    systems.append(dict(
                        bb=bb, d=d, rows=rows, el_rows=el_rows, lanes=slice(p * LANES, (p + 1) * LANES),
                        strict=strict, gcol=gcol, bcol=bcol, kg=kg, kb=kg * bcol,
                        dec=jnp.where(incl, jnp.exp(jnp.where(incl, gcol - grow, 0.0)), 0.0),
                        qg=qkv_s[bb, rows, p * LANES:(p + 1) * LANES],
                        vg=qkv_s[bb, rows, 2 * BR_WIDTH + p * LANES:2 * BR_WIDTH + (p + 1) * LANES],
                        kstack=_pair_stack(kg, lane2).astype(BF16)))
        for sy in systems:
            both = _dot_nt(jnp.concatenate([sy["kb"], sy["qg"]], axis=0).astype(BF16), sy["kstack"])
            sy["a"] = jnp.where(sy["strict"], both[0:c] * sy["dec"], 0.0)
            qk_ref[sy["bb"], sy["d"], sy["rows"], sy["lanes"]] = (both[c:2 * c] * sy["dec"]).astype(BF16)
        bd = lambda a: _pair_blockdiag(a, bdmask).astype(BF16)
        for sy in systems:
            pm = -jnp.where(ii // DN_INV_BLOCK == jj // DN_INV_BLOCK, sy["a"], 0.0)
            sy["t"] = eye2 + pm
            sy["pm"] = _dot(pm.astype(BF16), bd(pm))
        for sy in systems:
            both = _dot(jnp.concatenate([sy["t"], sy["pm"]], axis=0).astype(BF16), bd(sy["pm"]))
            sy["t"] = sy["t"] + both[0:c]
            sy["pm"] = both[c:2 * c]
        for sy in systems:
            sy["t"] = sy["t"] + _dot(sy["t"].astype(BF16), bd(sy["pm"]))
        size = DN_INV_BLOCK
        while size < c:
            coupling = (ii // (2 * size) == jj // (2 * size)) & (ii // size != jj // size)
            for sy in systems:
                sy["x"] = _dot(jnp.where(coupling, sy["a"], 0.0).astype(BF16), bd(sy["t"]))
            for sy in systems:
                sy["t"] = sy["t"] - _dot(sy["t"].astype(BF16), bd(sy["x"]))
            size *= 2
        for sy in systems:
            sy["eg"] = jnp.exp(sy["gcol"])
            rhs = jnp.concatenate([_pair_stack(sy["vg"] * sy["bcol"], lane2),
                                   _pair_stack(sy["kb"] * sy["eg"], lane2)], axis=1).astype(BF16)
            uw = _dot(sy["t"].astype(BF16), rhs)
            u_ref[sy["bb"], sy["d"], sy["rows"], sy["lanes"]] = uw[:, 0:LANES]
            w_ref[sy["bb"], sy["d"], sy["rows"], sy["lanes"]] = uw[:, LANES:2 * LANES].astype(BF16)
        for sy in systems:
            gcol = sy["gcol"]
            glast = gcol[c - 1:c, :] if sy["d"] == 0 else gcol[0:1, :]
            qd_ref[sy["bb"], sy["d"], sy["rows"], sy["lanes"]] = (sy["qg"] * sy["eg"]).astype(BF16)
            kd_ref[sy["bb"], sy["d"], sy["rows"], sy["lanes"]] = (sy["kg"] * jnp.exp(glast - gcol)).astype(BF16)
            el_ref[sy["bb"], sy["d"], sy["el_rows"], sy["lanes"]] = jnp.broadcast_to(jnp.exp(glast), (SUBLANES, LANES))
        return carry

    lax.fori_loop(0, tm // (c * DN_CHUNKS_PER_GROUP), chunk_group, 0)


def _dn_chunk(pd, pab, conv_w, alog, dtb, n_ctx):
    b, s, _ = pd.shape
    tm = TOKEN_TILE
    nt = s // tm
    nsub = tm // SUBLANES
    last_blk = s // SUBLANES - 1
    full = lambda a: pl.BlockSpec(a.shape, lambda i, j: (0,) * a.ndim)
    nb = _dn_batch_per_step(b)
    dir_tok = lambda: pl.BlockSpec((nb, 2, tm, BR_WIDTH), lambda i, j: (i, 0, j, 0))
    dir_shape = lambda dt: jax.ShapeDtypeStruct((b, 2, s, BR_WIDTH), dt)
    el_rows = (tm // DN_CHUNK) * SUBLANES
    return pl.pallas_call(
        functools.partial(_dn_chunk_body, n_ctx_tiles=n_ctx // tm, n_tiles=nt),
        grid=(b // nb, nt),
        in_specs=[pl.BlockSpec((nb, tm, PD_COLS), lambda i, j: (i, j, 0)),
                  pl.BlockSpec((nb, SUBLANES, PD_COLS), lambda i, j: (i, jnp.maximum(j * nsub - 1, 0), 0)),
                  pl.BlockSpec((nb, SUBLANES, PD_COLS),
                               lambda i, j: (i, jnp.minimum((j + 1) * nsub, last_blk), 0)),
                  pl.BlockSpec((nb, tm, PAB_COLS), lambda i, j: (i, j, 0)),
                  full(conv_w), full(alog), full(dtb)],
        out_specs=[dir_tok(), dir_tok(), dir_tok(), dir_tok(), dir_tok(),
                   pl.BlockSpec((nb, 2, el_rows, BR_WIDTH), lambda i, j: (i, 0, j, 0))],
        out_shape=[dir_shape(F32)] + [dir_shape(BF16)] * 4
        + [jax.ShapeDtypeStruct((b, 2, nt * el_rows, BR_WIDTH), F32)],
        scratch_shapes=[pltpu.VMEM((nb, tm, PD_COLS), F32), pltpu.VMEM((nb, tm, PAB_COLS), F32),
                        pltpu.VMEM((nb, tm, PAB_COLS), F32)],
        compiler_params=_cparams(("arbitrary", "arbitrary")),
        name="dn_chunk",
    )(pd, pd, pd, pab, conv_w, alog, dtb)


def _dn_scan_body(uf, wf, qkf, qdf, kdf, elf, ub, wb, qkb, qdb, kdb, elb, of_ref, ob_ref, s_ref):
    nb, tm = uf.shape[0], uf.shape[2]
    c = DN_CHUNK
    n_chunks = tm // c

    @pl.when(pl.program_id(1) == 0)
    def _():
        s_ref[...] = jnp.zeros_like(s_ref)

    bdmask = _iota((LANES, LANES), 0) // c == _iota((LANES, LANES), 1) // c
    views = ((uf, wf, qkf, qdf, kdf, elf, of_ref), (ub, wb, qkb, qdb, kdb, elb, ob_ref))
    chains = [(bb, d, p) for bb in range(nb) for d in range(2) for p in range(DN_HEADS // 2)]
    state = {ch: s_ref[ch] for ch in chains}
    for step in range(n_chunks):
        blk = {}
        for bb, d, p in chains:
            ck = step if d == 0 else n_chunks - 1 - step
            blk[bb, d, p] = (slice(ck * c, (ck + 1) * c), slice(p * LANES, (p + 1) * LANES), ck)
        tile = lambda k, ch: views[ch[1]][k][ch[0], 0, blk[ch][0], blk[ch][1]]
        sb = {ch: state[ch].astype(BF16) for ch in chains}
        ws = {ch: _dot(tile(1, ch), sb[ch]) for ch in chains}
        qs = {ch: _dot(tile(3, ch), sb[ch]) for ch in chains}
        vb = {ch: (tile(0, ch) - ws[ch]).astype(BF16) for ch in chains}
        for ch in chains:
            vbd = jnp.where(bdmask, jnp.concatenate([vb[ch], vb[ch]], axis=0), jnp.zeros((), BF16))
            views[ch[1]][6][ch[0], blk[ch][0], blk[ch][1]] = qs[ch] + _dot(tile(2, ch), vbd)
        upd = {ch: _dot_tn(tile(4, ch), vb[ch]) for ch in chains}
        for ch in chains:
            ck = blk[ch][2]
            el = views[ch[1]][5][ch[0], 0, ck * SUBLANES:ck * SUBLANES + 1, blk[ch][1]]
            state[ch] = state[ch] * el + jnp.where(bdmask, upd[ch], 0.0)
    for ch in chains:
        s_ref[ch] = state[ch]


def _dn_scan(u, w, qk, qd, kd, el, n_ctx):
    b, _, s, _ = u.shape
    tm = TOKEN_TILE
    nb = DN_SCAN_BATCH_PER_STEP if b % DN_SCAN_BATCH_PER_STEP == 0 else _dn_batch_per_step(b)
    nt = s // tm
    nctx = n_ctx // tm
    el_rows = (tm // DN_CHUNK) * SUBLANES

    def rev(j):
        return jnp.where(j < nctx, nctx - 1 - j, nt - 1 - (j - nctx))

    fwd = lambda rows: pl.BlockSpec((nb, 1, rows, BR_WIDTH), lambda i, j: (i, 0, j, 0))
    bwd = lambda rows: pl.BlockSpec((nb, 1, rows, BR_WIDTH), lambda i, j: (i, 1, rev(j), 0))
    in_specs = [fwd(tm)] * 5 + [fwd(el_rows)] + [bwd(tm)] * 5 + [bwd(el_rows)]
    out_shape = jax.ShapeDtypeStruct((b, s, BR_WIDTH), F32)
    return pl.pallas_call(
        _dn_scan_body,
        grid=(b // nb, nt),
        in_specs=in_specs,
        out_specs=[pl.BlockSpec((nb, tm, BR_WIDTH), lambda i, j: (i, j, 0)),
                   pl.BlockSpec((nb, tm, BR_WIDTH), lambda i, j: (i, rev(j), 0))],
        out_shape=[out_shape, out_shape],
        scratch_shapes=[pltpu.VMEM((nb, 2, DN_HEADS // 2, LANES, LANES), F32)],
        compiler_params=_cparams(("arbitrary", "arbitrary")),
        name="dn_scan",
    )(u, w, qk, qd, kd, el, u, w, qk, qd, kd, el)


def _merge_body(ctx_ref, x_ref, mod_ref, h_ref, pg_ref, ya_ref, yb_ref, yc_ref, of_ref, ob_ref, dng_ref,
                wg_ref, wb_ref, wo_ref, lng_ref, lnb_ref, *o_refs, alpha, n_ctx_tiles, skip_tiles):
    is_ctx = pl.program_id(1) + skip_tiles < n_ctx_tiles
    hb = h_ref[0]
    o = of_ref[0] + ob_ref[0]
    ms = _sum_dot(o * o, _block_ones(BR_WIDTH, DN_DV).astype(BF16), 2) * (1.0 / DN_DV)
    yd = o * lax.rsqrt(ms + LN_EPS) * dng_ref[...]
    pg = pg_ref[0]
    sg = pg * _sigmoid(pg)
    acc = None
    for i, y in enumerate((ya_ref[0], yb_ref[0], yc_ref[0], yd)):
        t = (y * sg[:, i * BR_WIDTH:(i + 1) * BR_WIDTH]).astype(BF16)
        term = _sigmoid(_dot(hb, wg_ref[i])) * _dot(t, wb_ref[i])
        acc = term if acc is None else acc + term
    out = _dot(acc.astype(BF16), wo_ref[...])
    gt = mod_ref[0, 0, 2:3, :]
    r = alpha * jnp.where(is_ctx, ctx_ref[0], x_ref[0]) + gt * out
    mu = jnp.mean(r, -1, keepdims=True)
    rc = r - mu
    var = jnp.mean(rc * rc, -1, keepdims=True)
    res = rc * lax.rsqrt(var + LN_EPS) * lng_ref[...] + lnb_ref[...]
    if len(o_refs) == 1:
        o_refs[0][0] = res
    else:
        @pl.when(is_ctx)
        def _():
            o_refs[0][0] = res

        @pl.when(jnp.logical_not(is_ctx))
        def _():
            o_refs[1][0] = res


def _merge(ctx_s, x_s, mod, h, pg, ya, yb, yc, o_f, o_b, dng, wg, wb, wo, lng, lnb, last, alpha):
    b, n_ctx, d = ctx_s.shape
    n = x_s.shape[1]
    tm = TOKEN_TILE
    n_ctx_tiles = n_ctx // tm
    skip_tiles = n_ctx_tiles if last else 0
    nt = (n_ctx + n) // tm - skip_tiles
    tok = lambda width: pl.BlockSpec((1, tm, width), lambda i, j: (i, j + skip_tiles, 0))
    full = lambda a: pl.BlockSpec(a.shape, lambda i, j: (0,) * a.ndim)
    x_out = jax.ShapeDtypeStruct((b, n, d), F32)
    if last:
        out_specs = [pl.BlockSpec((1, tm, d), lambda i, j: (i, j, 0))]
        out_shape = [x_out]
    else:
        out_specs = _stream_specs(tm, d, n_ctx_tiles)
        out_shape = [jax.ShapeDtypeStruct((b, n_ctx, d), F32), x_out]
    return pl.pallas_call(
        functools.partial(_merge_body, alpha=alpha, n_ctx_tiles=n_ctx_tiles, skip_tiles=skip_tiles),
        grid=(b, nt),
        in_specs=_stream_specs(tm, d, n_ctx_tiles, skip_tiles) + [
                  pl.BlockSpec((1, 1, 3, d),
                               lambda i, j: (i, jnp.where(j + skip_tiles >= n_ctx_tiles, 1, 0), 0, 0)),
                  tok(d), tok(PG_COLS), tok(BR_WIDTH), tok(BR_WIDTH), tok(BR_WIDTH), tok(BR_WIDTH),
                  tok(BR_WIDTH), full(dng), full(wg), full(wb), full(wo), full(lng), full(lnb)],
        out_specs=out_specs,
        out_shape=out_shape,
        compiler_params=_cparams(("arbitrary", "arbitrary")),
        name="merge",
    )(ctx_s, x_s, mod, h, pg, ya, yb, yc, o_f, o_b, dng, wg, wb, wo, lng, lnb)


def _rope_swap_index(n_blocks):
    blk = jnp.array(list(range(8, 16)) + list(range(0, 8)) + list(range(24, 32)) + list(range(16, 24)))
    return (jnp.arange(n_blocks)[:, None] * 32 + blk[None, :]).reshape(-1)


def _pack_w_in(w):
    d = w.shape[0]
    o = 0
    cq = w[:, o:o + MLA_Q_LORA]; o += MLA_Q_LORA
    ckv = w[:, o:o + MLA_KV_LORA]; o += MLA_KV_LORA
    kr = w[:, o:o + MLA_ROPE]; o += MLA_ROPE
    pb = w[:, o:o + GMLP_COLS]; o += GMLP_COLS
    dq = w[:, o:o + 256]; dk = w[:, o + 256:o + 512]; dv = w[:, o + 512:o + 768]; o += DIFF_COLS
    dn_qkv = w[:, o:o + 3 * BR_WIDTH]; dn_ab = w[:, o + 3 * BR_WIDTH:o + DN_COLS]; o += DN_COLS
    pg = w[:, o:]
    z = lambda n: jnp.zeros((d, n), w.dtype)
    sw1 = _rope_swap_index(1)
    sw8 = _rope_swap_index(8)
    place = lambda a: jnp.concatenate([z(MLA_NOPE), a, z(LANES - MLA_NOPE - MLA_ROPE)], axis=1)
    packed = jnp.concatenate(
        [cq, ckv, place(kr), place(kr[:, sw1]), pb, dq, dk, dv, dq[:, sw8], dk[:, sw8], dn_qkv,
         dn_ab, z(PAB_COLS - 4 * DN_HEADS), pg], axis=1)
    return packed.astype(BF16)


def _pack_mla_weights(w_uq, w_ukv):
    dq = MLA_NOPE + MLA_ROPE
    wq = w_uq.reshape(MLA_Q_LORA, MLA_HEADS, dq)
    zq = jnp.zeros((MLA_Q_LORA, MLA_HEADS, LANES - dq), w_uq.dtype)
    wq_p = jnp.concatenate([wq, zq], axis=2).reshape(MLA_Q_LORA, MLA_HEADS * LANES)
    rope_sw = wq[:, :, MLA_NOPE:][:, :, _rope_swap_index(1)]
    wqs_p = jnp.concatenate([jnp.zeros((MLA_Q_LORA, MLA_HEADS, MLA_NOPE), w_uq.dtype), rope_sw, zq],
                            axis=2).reshape(MLA_Q_LORA, MLA_HEADS * LANES)
    wkv = w_ukv.reshape(MLA_KV_LORA, MLA_HEADS, MLA_NOPE + MLA_V)
    zk = jnp.zeros((MLA_KV_LORA, MLA_HEADS, LANES - MLA_NOPE), w_ukv.dtype)
    wk_p = jnp.concatenate([wkv[:, :, :MLA_NOPE], zk], axis=2).reshape(MLA_KV_LORA, MLA_HEADS * LANES)
    wv_p = wkv[:, :, MLA_NOPE:].reshape(MLA_KV_LORA, MLA_HEADS * MLA_V)
    return wq_p.astype(BF16), wqs_p.astype(BF16), wk_p.astype(BF16), wv_p.astype(BF16)


def _rope_tables(n, n_ctx):
    rows = n // GRID_W
    row = jnp.repeat(jnp.arange(rows, dtype=F32), GRID_W)
    col = jnp.tile(jnp.arange(GRID_W, dtype=F32), rows)
    axis_dim = MLA_ROPE // 2
    inv_freq = ROPE_BASE ** (-jnp.arange(0, axis_dim, 2, dtype=F32) / axis_dim)
    ar = row[:, None] * inv_freq
    ac = col[:, None] * inv_freq
    cos32 = jnp.concatenate([jnp.cos(ar), jnp.cos(ar), jnp.cos(ac), jnp.cos(ac)], axis=1)
    sin32 = jnp.concatenate([-jnp.sin(ar), jnp.sin(ar), -jnp.sin(ac), jnp.sin(ac)], axis=1)
    cos32 = jnp.concatenate([jnp.ones((n_ctx, 32), F32), cos32], axis=0)
    sin32 = jnp.concatenate([jnp.zeros((n_ctx, 32), F32), sin32], axis=0)
    s = n + n_ctx
    one = lambda w: jnp.ones((s, w), F32)
    zero = lambda w: jnp.zeros((s, w), F32)
    mla_cos = jnp.concatenate([one(MLA_NOPE), cos32, one(LANES - MLA_NOPE - MLA_ROPE)], axis=1)
    mla_sin = jnp.concatenate([zero(MLA_NOPE), sin32, zero(LANES - MLA_NOPE - MLA_ROPE)], axis=1)
    diff_cos = jnp.tile(cos32, (1, LANES // 32))
    diff_sin = jnp.tile(sin32, (1, LANES // 32))
    return mla_cos, mla_sin, diff_cos, diff_sin


def _pad_lanes(a, width=LANES):
    return jnp.concatenate([a, jnp.zeros(a.shape[:-1] + (width - a.shape[-1],), a.dtype)], axis=-1)


def kernel(x, c, ctx, c_ctx, w_mod, b_mod, w_in, mla_q_norm, mla_w_uq, mla_kv_norm, mla_w_ukv, gmlp_ln_g, gmlp_w_s, gmlp_b_s, diff_lq1, diff_lk1, diff_lq2, diff_lk2, diff_norm_g, dn_conv_w, dn_a_log, dn_dt_bias, dn_norm_g, w_gate, w_branch, w_out, ln_g, ln_b):
    b, n, d = x.shape
    n_ctx = ctx.shape[1]
    depth = w_mod.shape[0]
    tm = TOKEN_TILE
    assert d == D_MODEL and n % tm == 0 and n_ctx % tm == 0 and n % GRID_W == 0
    assert tm == ATTN_Q_TILE == ATTN_K_TILE
    alpha = (2 * depth) ** 0.25
    n_ctx_tiles = n_ctx // tm

    ctx_s, x_s = ctx, x
    mla_cos, mla_sin, diff_cos, diff_sin = _rope_tables(n, n_ctx)

    rows = ((b + 1 + SUBLANES - 1) // SUBLANES) * SUBLANES
    cc = jnp.concatenate([c, c_ctx[None, :], jnp.zeros((rows - b - 1, d), F32)], axis=0)
    mod_all = _modulation(cc, w_mod, b_mod)

    for l in range(depth):
        last = l == depth - 1
        lam_init = 0.8 - 0.6 * math.exp(-0.3 * l)
        mod_l = mod_all[l].reshape(rows, 3, d)
        mod = jnp.stack([jnp.broadcast_to(mod_l[b][None], (b, 3, d)), mod_l[:b]], axis=1)

        wq, wqs, wk, wv = _pack_mla_weights(mla_w_uq[l], mla_w_ukv[l])
        bias = jnp.repeat(gmlp_b_s[l].T, BR_WIDTH // GMLP_GROUPS, axis=1)
        h, pd, pab, pg, q_a, k_a, vt_a, yb, q_c, k_c, vt_c = _inproj(
            ctx_s, x_s, mod, _pack_w_in(w_in[l]), (mla_cos, mla_sin, diff_cos, diff_sin),
            (mla_q_norm[l][None, :], mla_kv_norm[l][None, :], wq, wqs, wk, wv),
            (gmlp_ln_g[l][None, :], gmlp_w_s[l].astype(BF16), bias))
        ya = _mla_attn(q_a, k_a, vt_a, n_ctx)
        lqk = _pad_lanes(jnp.stack([diff_lq1[l], diff_lk1[l], diff_lq2[l], diff_lk2[l]], axis=0))
        yc = _diff_attn(q_c, k_c, vt_c, lqk, diff_norm_g[l][:, None], n_ctx, lam_init)

        conv_w = jnp.concatenate([dn_conv_w[l], jnp.zeros((SUBLANES - 3, PD_COLS), F32)], axis=0)
        alog = _pad_lanes(dn_a_log[l].reshape(1, 2 * DN_HEADS))
        dtb = _pad_lanes(dn_dt_bias[l].reshape(1, 2 * DN_HEADS))
        u, w, qk, qd, kd, el = _dn_chunk(pd, pab, conv_w, alog, dtb, n_ctx)
        o_f, o_b = _dn_scan(u, w, qk, qd, kd, el, n_ctx)

        dng = jnp.tile(dn_norm_g[l], DN_HEADS)[None, :]
        outs = _merge(ctx_s, x_s, mod, h, pg, ya, yb, yc, o_f, o_b, dng, w_gate[l].astype(BF16),
                      w_branch[l].astype(BF16), w_out[l].astype(BF16), ln_g[l][None, :], ln_b[l][None, :],
                      last, alpha)
        if last:
            return outs[0]
        ctx_s, x_s = outs
```
